```python
import jax
import jax.numpy as jnp
from jax import lax
import numpy as np

D_MODEL = 1024
BATCH = 16
SEQ = 2048
DEPTH = 4

GRID_W = 64
CTX_LEN = 256
HEAD_DIM = 64
RWKV_HEADS = 4
ATTN_HEADS = 8
ATTN_KV_HEADS = 2
ATTN_GROUP = ATTN_HEADS // ATTN_KV_HEADS
MLSTM_HEADS = 4
C_R = RWKV_HEADS * HEAD_DIM
C_A = ATTN_HEADS * HEAD_DIM
C_KV = ATTN_KV_HEADS * HEAD_DIM
C_M = MLSTM_HEADS * HEAD_DIM
D_MIX = C_R + C_A + C_M
R_W = 64
R_A = 64
R_G = 128
RWKV_LORA = 2 * R_W + 2 * R_A + R_G
IN_SPLITS = (3 * C_R, RWKV_LORA, C_A + 2 * C_KV, 2 * C_M, 2 * C_M, 4 * MLSTM_HEADS)
N_IN = 3 * C_R + RWKV_LORA + C_A + 2 * C_KV + 4 * C_M + 4 * MLSTM_HEADS
D_FF = 4 * D_MODEL
CONV_W = 3
WINDOW = 128
ATTN_BLOCK = 128
MLSTM_CHUNK = 64
ROPE_BASE = 10000.0
ROPE_FREQS = HEAD_DIM // 4
HALF = HEAD_DIM // 2
DECAY_SCALE = 0.606531
RMS_EPS = 1e-6
RWKV_LN_EPS = 64e-5
MLSTM_LN_EPS = 1e-6
NEG_INF = -1e30

kernel_name = 'hybrid_rwkv7_swa_mlstm_dit_block'


def rms_norm(x, g):
    xf = x.astype(jnp.float32)
    y = xf * lax.rsqrt(jnp.mean(xf * xf, axis=-1, keepdims=True) + RMS_EPS)
    return (y * g.astype(jnp.float32)).astype(x.dtype)


def head_layernorm(y, w, eps):
    mu = jnp.mean(y, axis=-1, keepdims=True)
    yc = y - mu
    var = jnp.mean(yc * yc, axis=-1, keepdims=True)
    return yc * lax.rsqrt(var + eps) * w.astype(jnp.float32).reshape(y.shape[-2], y.shape[-1])


def split_heads(u):
    return u.reshape(u.shape[0], u.shape[1], -1, HEAD_DIM)


def centred_conv(u, w):
    half = CONV_W // 2
    length = u.shape[1]
    up = jnp.pad(u, ((0, 0), (half, half), (0, 0)))
    out = up[:, 0:length] * w[0]
    for j in range(1, CONV_W):
        out = out + up[:, j:j + length] * w[j]
    return out


def axial_rope(rows):
    row = jnp.repeat(jnp.arange(rows), GRID_W).astype(jnp.float32)
    col = jnp.tile(jnp.arange(GRID_W), rows).astype(jnp.float32)
    inv = ROPE_BASE ** (-jnp.arange(ROPE_FREQS, dtype=jnp.float32) / ROPE_FREQS)
    ang = jnp.concatenate([row[:, None] * inv, col[:, None] * inv], axis=-1)
    return jnp.cos(ang), jnp.sin(ang)


def apply_rope(u, cos, sin):
    shape = (1, cos.shape[0]) + (1,) * (u.ndim - 3) + (HALF,)
    cs = cos.reshape(shape)
    sn = sin.reshape(shape)
    u1 = u[..., :HALF]
    u2 = u[..., HALF:]
    return jnp.concatenate([u1 * cs - u2 * sn, u1 * sn + u2 * cs], axis=-1).astype(u.dtype)


def split_projection(p):
    points = [int(i) for i in np.cumsum(IN_SPLITS)[:-1]]
    return jnp.split(p, points, axis=-1)


def rwkv7_scan(s0, r, w, k, v, kk, b, reverse):
    xs = tuple(jnp.moveaxis(u, 1, 0) for u in (r, w, k, v, kk, b))

    def step(s, inp):
        r_t, w_t, k_t, v_t, kk_t, b_t = inp
        sa = jnp.einsum('bhvk,bhk->bhv', s, kk_t)
        s = s * w_t[:, :, None, :] - sa[..., None] * b_t[:, :, None, :] + v_t[..., None] * k_t[:, :, None, :]
        return s, jnp.einsum('bhvk,bhk->bhv', s, r_t)

    s_final, y = lax.scan(step, s0, xs, reverse=reverse)
    return s_final, jnp.moveaxis(y, 0, 1)


def rwkv_mixer(rkv_c, z_c, rkv_l, z_l, conv_w, w0, w_up, a0, a_up, g_up, k_k, k_a, r_k, ln_w, ln_b, need_ctx):
    out_dtype = rkv_l.dtype

    def prep(rkv, z):
        rkv = centred_conv(rkv, conv_w).astype(jnp.float32)
        r, k, v = jnp.split(rkv, 3, axis=-1)
        zw, za, zg = jnp.split(z.astype(jnp.float32), [2 * R_W, 2 * R_W + 2 * R_A], axis=-1)
        kk = split_heads(k * k_k)
        kk = kk * lax.rsqrt(jnp.sum(kk * kk, axis=-1, keepdims=True) + 1e-12)
        per_dir = []
        for d in range(2):
            dw = w0[d] + jnp.tanh(zw[..., d * R_W:(d + 1) * R_W]) @ w_up[d]
            w = jnp.exp(-DECAY_SCALE * jax.nn.sigmoid(dw))
            a = jax.nn.sigmoid(a0[d] + za[..., d * R_A:(d + 1) * R_A] @ a_up[d])
            k_dir = k * (1.0 + (a - 1.0) * k_a)
            per_dir.append((split_heads(w), split_heads(k_dir), kk * split_heads(a)))
        g = jax.nn.sigmoid(zg) @ g_up
        return split_heads(r), split_heads(k), split_heads(v), kk, g, per_dir

    r_c, k_c, v_c, kk_c, g_c, dir_c = prep(rkv_c, z_c)
    r_l, k_l, v_l, kk_l, g_l, dir_l = prep(rkv_l, z_l)
    s0 = jnp.zeros((r_l.shape[0], RWKV_HEADS, HEAD_DIM, HEAD_DIM), jnp.float32)
    y_c = 0.0
    y_l = 0.0
    for d in range(2):
        s_ctx, yc = rwkv7_scan(s0, r_c, dir_c[d][0], dir_c[d][1], v_c, kk_c, dir_c[d][2], d == 1)
        _, yl = rwkv7_scan(s_ctx, r_l, dir_l[d][0], dir_l[d][1], v_l, kk_l, dir_l[d][2], d == 1)
        y_c = y_c + yc
        y_l = y_l + yl

    def post(y, r, k, v, g):
        y = head_layernorm(y, ln_w, RWKV_LN_EPS) + ln_b.astype(jnp.float32).reshape(RWKV_HEADS, HEAD_DIM)
        bonus = jnp.sum(r * k * r_k.reshape(RWKV_HEADS, HEAD_DIM), axis=-1, keepdims=True) * v
        return ((y + bonus).reshape(y.shape[0], y.shape[1], C_R) * g).astype(out_dtype)

    out_c = post(y_c, r_c, k_c, v_c, g_c) if need_ctx else None
    return out_c, post(y_l, r_l, k_l, v_l, g_l)


def attention_mixer(qkv_c, qkv_l, sink, cos, sin, need_ctx):
    out_dtype = qkv_l.dtype
    scale = HEAD_DIM ** -0.5

    def heads(qkv):
        q, k, v = jnp.split(qkv, [C_A, C_A + C_KV], axis=-1)
        b, n = q.shape[:2]
        return (q.reshape(b, n, ATTN_KV_HEADS, ATTN_GROUP, HEAD_DIM),
                k.reshape(b, n, ATTN_KV_HEADS, HEAD_DIM),
                v.reshape(b, n, ATTN_KV_HEADS, HEAD_DIM))

    q_c, k_c, v_c = heads(qkv_c)
    q_l, k_l, v_l = heads(qkv_l)
    q_l = apply_rope(q_l, cos, sin)
    k_l = apply_rope(k_l, cos, sin)
    B, S = q_l.shape[:2]
    T = k_c.shape[1]
    nb = S // ATTN_BLOCK
    sink = sink.astype(jnp.float32).reshape(1, ATTN_KV_HEADS, ATTN_GROUP, 1, 1)

    pad = ((0, 0), (ATTN_BLOCK, ATTN_BLOCK), (0, 0), (0, 0))

    def band(u):
        u = jnp.pad(u, pad).reshape(B, nb + 2, ATTN_BLOCK, ATTN_KV_HEADS, HEAD_DIM)
        u = jnp.concatenate([u[:, :-2], u[:, 1:-1], u[:, 2:]], axis=2)
        return jnp.moveaxis(u, 1, 0)

    k_band = band(k_l)
    v_band = band(v_l)
    q_blk = jnp.moveaxis(q_l.reshape(B, nb, ATTN_BLOCK, ATTN_KV_HEADS, ATTN_GROUP, HEAD_DIM), 1, 0)
    blk = jnp.arange(nb)[:, None, None] * ATTN_BLOCK
    q_pos = blk + jnp.arange(ATTN_BLOCK)[None, :, None]
    k_pos = blk - ATTN_BLOCK + jnp.arange(3 * ATTN_BLOCK)[None, None, :]
    valid = (k_pos >= 0) & (k_pos < S) & (jnp.abs(k_pos - q_pos) <= WINDOW)
    sink_l = jnp.broadcast_to(sink, (B, ATTN_KV_HEADS, ATTN_GROUP, ATTN_BLOCK, 1))

    def block(args):
        q_i, k_i, v_i, m_i = args
        s_ctx = jnp.einsum('bqhgd,bkhd->bhgqk', q_i, k_c).astype(jnp.float32) * scale
        s_loc = jnp.einsum('bqhgd,bkhd->bhgqk', q_i, k_i).astype(jnp.float32) * scale
        s_loc = jnp.where(m_i, s_loc, NEG_INF)
        p = jax.nn.softmax(jnp.concatenate([sink_l, s_ctx, s_loc], axis=-1), axis=-1)
        o = jnp.einsum('bhgqk,bkhd->bqhgd', p[..., 1:1 + T].astype(v_c.dtype), v_c)
        return o + jnp.einsum('bhgqk,bkhd->bqhgd', p[..., 1 + T:].astype(v_i.dtype), v_i)

    y_l = jnp.moveaxis(lax.map(block, (q_blk, k_band, v_band, valid)), 0, 1).reshape(B, S, C_A)

    y_c = None
    if need_ctx:
        s = jnp.einsum('bqhgd,bkhd->bhgqk', q_c, k_c).astype(jnp.float32) * scale
        sink_c = jnp.broadcast_to(sink, (B, ATTN_KV_HEADS, ATTN_GROUP, T, 1))
        p = jax.nn.softmax(jnp.concatenate([sink_c, s], axis=-1), axis=-1)[..., 1:]
        y_c = jnp.einsum('bhgqk,bkhd->bqhgd', p.astype(v_c.dtype), v_c).reshape(B, T, C_A).astype(out_dtype)
    return y_c, y_l.astype(out_dtype)


def mlstm_chunkwise(state, q, k, v, ig, lf):
    B, H, L, N = q.shape
    nc = L // MLSTM_CHUNK

    def chunks(u):
        u = u.reshape((B, H, nc, MLSTM_CHUNK) + u.shape[3:])
        return jnp.moveaxis(u, 2, 0)

    tri = jnp.tril(jnp.ones((MLSTM_CHUNK, MLSTM_CHUNK), bool))

    def step(carry, inp):
        C, n, m = carry
        q_c, k_c, v_c, i_c, f_c = inp
        b = jnp.cumsum(f_c, axis=-1)
        logd = jnp.where(tri, b[..., :, None] - b[..., None, :] + i_c[..., None, :], NEG_INF)
        inter = b + m[..., None]
        m_t = jnp.maximum(inter, jnp.max(logd, axis=-1))
        w_intra = jnp.exp(logd - m_t[..., None])
        w_inter = jnp.exp(inter - m_t)
        s = jnp.einsum('bhtd,bhsd->bhts', q_c, k_c) * w_intra
        num = jnp.einsum('bhts,bhsd->bhtd', s, v_c) + w_inter[..., None] * jnp.einsum('bhvk,bhtk->bhtv', C, q_c)
        den = jnp.sum(s, axis=-1) + w_inter * jnp.einsum('bhk,bhtk->bht', n, q_c)
        h = num / jnp.maximum(jnp.abs(den), jnp.exp(-m_t))[..., None]
        b_end = b[..., -1]
        g = b_end[..., None] - b + i_c
        m_new = jnp.maximum(b_end + m, jnp.max(g, axis=-1))
        gw = jnp.exp(g - m_new[..., None])
        dec = jnp.exp(b_end + m - m_new)
        C = dec[..., None, None] * C + jnp.einsum('bhs,bhsv,bhsk->bhvk', gw, v_c, k_c)
        n = dec[..., None] * n + jnp.einsum('bhs,bhsk->bhk', gw, k_c)
        return (C, n, m_new), h

    carry, h = lax.scan(step, state, tuple(chunks(u) for u in (q, k, v, ig, lf)))
    h = jnp.moveaxis(h, 0, 2).reshape(B, H, L, N)
    return carry, h


def mlstm_mixer(qk_c, vo_c, gt_c, qk_l, vo_l, gt_l, conv_w, b_i, b_f, norm_g, need_ctx):
    out_dtype = qk_l.dtype

    def to_bhln(u):
        return jnp.swapaxes(split_heads(u), 1, 2)

    def prep(qk, vo, gt):
        qk = jax.nn.silu(centred_conv(qk, conv_w).astype(jnp.float32))
        q, k = jnp.split(qk, 2, axis=-1)
        v, o = jnp.split(vo.astype(jnp.float32), 2, axis=-1)
        zi, zf = jnp.split(gt.astype(jnp.float32), 2, axis=-1)
        return to_bhln(q), to_bhln(k) * HEAD_DIM ** -0.5, to_bhln(v), o, zi, zf

    q_c, k_c, v_c, o_c, zi_c, zf_c = prep(qk_c, vo_c, gt_c)
    q_l, k_l, v_l, o_l, zi_l, zf_l = prep(qk_l, vo_l, gt_l)
    B = q_l.shape[0]
    zero = (jnp.zeros((B, MLSTM_HEADS, HEAD_DIM, HEAD_DIM), jnp.float32),
            jnp.zeros((B, MLSTM_HEADS, HEAD_DIM), jnp.float32),
            jnp.zeros((B, MLSTM_HEADS), jnp.float32))
    h_c = 0.0
    h_l = 0.0
    for d in range(2):
        sl = slice(d * MLSTM_HEADS, (d + 1) * MLSTM_HEADS)
        i_c = jnp.swapaxes(zi_c[..., sl] + b_i[d], 1, 2)
        f_c = jax.nn.log_sigmoid(jnp.swapaxes(zf_c[..., sl] + b_f[d], 1, 2))
        i_l = jnp.swapaxes(zi_l[..., sl] + b_i[d], 1, 2)
        f_l = jax.nn.log_sigmoid(jnp.swapaxes(zf_l[..., sl] + b_f[d], 1, 2))
        if d == 1:
            seq_c = [jnp.flip(u, 2) for u in (q_c, k_c, v_c, i_c, f_c)]
            seq_l = [jnp.flip(u, 2) for u in (q_l, k_l, v_l, i_l, f_l)]
        else:
            seq_c = [q_c, k_c, v_c, i_c, f_c]
            seq_l = [q_l, k_l, v_l, i_l, f_l]
        state_c, hc = mlstm_chunkwise(zero, *seq_c)
        _, hl = mlstm_chunkwise(state_c, *seq_l)
        if d == 1:
            hc = jnp.flip(hc, 2)
            hl = jnp.flip(hl, 2)
        h_c = h_c + hc
        h_l = h_l + hl

    def post(h, o):
        h = head_layernorm(jnp.swapaxes(h, 1, 2), norm_g, MLSTM_LN_EPS)
        return (jax.nn.sigmoid(o) * h.reshape(h.shape[0], h.shape[1], C_M)).astype(out_dtype)

    out_c = post(h_c, o_c) if need_ctx else None
    return out_c, post(h_l, o_l)


def sq_relu_mlp(h, w1, w2):
    return jnp.square(jax.nn.relu(h @ w1)) @ w2


def setup_inputs(seed: int = 0) -> dict:
    key = jax.random.key(seed)
    ks = iter(jax.random.split(key, 32))
    f32 = jnp.float32

    def nrm(shape, scale=1.0):
        return jax.random.normal(next(ks), shape, f32) * scale

    L = DEPTH
    D = D_MODEL
    centre = jnp.zeros((CONV_W, 1), f32).at[CONV_W // 2].set(1.0)
    return {
        'x': nrm((BATCH, SEQ, D)),
        'c': nrm((BATCH, D)),
        'ctx': nrm((BATCH, CTX_LEN, D)),
        'c_ctx': nrm((D,)),
        'ada_w': nrm((L, D, 6 * D), 0.5 * D ** -0.5),
        'ada_b': nrm((L, 6 * D), 0.02),
        'norm1_g': 1.0 + nrm((L, D), 0.02),
        'norm2_g': 1.0 + nrm((L, D), 0.02),
        'w_in': nrm((L, D, N_IN), D ** -0.5),
        'w_out': nrm((L, D_MIX, D), D_MIX ** -0.5),
        'rwkv_conv': centre + nrm((L, CONV_W, 3 * C_R), 0.2),
        'rwkv_w0': nrm((L, 2, C_R), 1.0),
        'rwkv_w_up': nrm((L, 2, R_W, C_R), R_W ** -0.5),
        'rwkv_a0': nrm((L, 2, C_R), 0.5),
        'rwkv_a_up': nrm((L, 2, R_A, C_R), 0.5 * R_A ** -0.5),
        'rwkv_g_up': nrm((L, R_G, C_R), R_G ** -0.5),
        'rwkv_k_k': 1.0 + nrm((L, C_R), 0.1),
        'rwkv_k_a': 1.0 + nrm((L, C_R), 0.1),
        'rwkv_r_k': nrm((L, C_R), 0.1),
        'rwkv_ln_w': 1.0 + nrm((L, C_R), 0.02),
        'rwkv_ln_b': nrm((L, C_R), 0.02),
        'attn_sink': nrm((L, ATTN_HEADS), 0.5),
        'mlstm_conv': centre + nrm((L, CONV_W, 2 * C_M), 0.2),
        'mlstm_b_i': nrm((L, 2, MLSTM_HEADS), 0.1),
        'mlstm_b_f': jnp.linspace(3.0, 6.0, MLSTM_HEADS, dtype=f32) + nrm((L, 2, MLSTM_HEADS), 0.1),
        'mlstm_norm_g': 1.0 + nrm((L, C_M), 0.02),
        'mlp_w1': nrm((L, D, D_FF), D ** -0.5),
        'mlp_w2': nrm((L, D_FF, D), D_FF ** -0.5),
        'final_g': 1.0 + nrm((D,), 0.02),
    }


def reference(x, c, ctx, c_ctx, ada_w, ada_b, norm1_g, norm2_g, w_in, w_out,
              rwkv_conv, rwkv_w0, rwkv_w_up, rwkv_a0, rwkv_a_up, rwkv_g_up,
              rwkv_k_k, rwkv_k_a, rwkv_r_k, rwkv_ln_w, rwkv_ln_b,
              attn_sink, mlstm_conv, mlstm_b_i, mlstm_b_f, mlstm_norm_g,
              mlp_w1, mlp_w2, final_g):
    S = x.shape[1]
    rows = S // GRID_W
    cos, sin = axial_rope(rows)
    x_l = x
    x_c = ctx
    silu_c = jax.nn.silu(c)
    silu_cc = jax.nn.silu(c_ctx)[None, :]
    for l in range(DEPTH):
        need_ctx = l < DEPTH - 1
        mod_l = jnp.split((silu_c @ ada_w[l] + ada_b[l])[:, None, :], 6, axis=-1)
        mod_c = jnp.split((silu_cc @ ada_w[l] + ada_b[l])[:, None, :], 6, axis=-1)
        h_l = rms_norm(x_l, norm1_g[l]) * (1.0 + mod_l[1]) + mod_l[0]
        h_c = rms_norm(x_c, norm1_g[l]) * (1.0 + mod_c[1]) + mod_c[0]
        p_l = split_projection(h_l @ w_in[l])
        p_c = split_projection(h_c @ w_in[l])
        yr_c, yr_l = rwkv_mixer(p_c[0], p_c[1], p_l[0], p_l[1], rwkv_conv[l], rwkv_w0[l], rwkv_w_up[l],
                                rwkv_a0[l], rwkv_a_up[l], rwkv_g_up[l], rwkv_k_k[l], rwkv_k_a[l],
                                rwkv_r_k[l], rwkv_ln_w[l], rwkv_ln_b[l], need_ctx)
        ya_c, ya_l = attention_mixer(p_c[2], p_l[2], attn_sink[l], cos, sin, need_ctx)
        ym_c, ym_l = mlstm_mixer(p_c[3], p_c[4], p_c[5], p_l[3], p_l[4], p_l[5], mlstm_conv[l],
                                 mlstm_b_i[l], mlstm_b_f[l], mlstm_norm_g[l], need_ctx)
        x_l = x_l + mod_l[2] * (jnp.concatenate([yr_l, ya_l, ym_l], axis=-1) @ w_out[l])
        x_l = x_l + mod_l[5] * sq_relu_mlp(rms_norm(x_l, norm2_g[l]) * (1.0 + mod_l[4]) + mod_l[3],
                                           mlp_w1[l], mlp_w2[l])
        if need_ctx:
            x_c = x_c + mod_c[2] * (jnp.concatenate([yr_c, ya_c, ym_c], axis=-1) @ w_out[l])
            x_c = x_c + mod_c[5] * sq_relu_mlp(rms_norm(x_c, norm2_g[l]) * (1.0 + mod_c[4]) + mod_c[3],
                                               mlp_w1[l], mlp_w2[l])
    return rms_norm(x_l, final_g)
```

```python
import functools

import jax
import jax.numpy as jnp
from jax import lax
from jax.experimental import pallas as pl
from jax.experimental.pallas import tpu as pltpu

F32 = jnp.float32
BF16 = jnp.bfloat16

HEAD_DIM = 64
N_HEADS = 4
C_MIX = N_HEADS * HEAD_DIM
ATTN_KV_HEADS = 2
ATTN_GROUP = 4
C_A = ATTN_KV_HEADS * ATTN_GROUP * HEAD_DIM
C_KV = ATTN_KV_HEADS * HEAD_DIM
GRID_W = 64
ATTN_BLOCK = 128
WINDOW = 128
ROPE_BASE = 10000.0
R_LORA = 64
R_GATE = 128
DECAY_SCALE = 0.606531
RMS_EPS = 1e-6
RWKV_LN_EPS = 64e-5
MLSTM_LN_EPS = 1e-6
NEG_INF = -1e30
CHUNK = 64
HALO = 8
VMEM_LIMIT = 56 * 1024 * 1024


def _dot(a, b):
    return jnp.dot(a, b, preferred_element_type=F32)


def _dot_nt(a, b):
    return lax.dot_general(a, b, (((1,), (1,)), ((), ())), preferred_element_type=F32)


def _dot_tn(a, b):
    return lax.dot_general(a, b, (((0,), (0,)), ((), ())), preferred_element_type=F32)


def _split2(a):
    hi = a.astype(BF16)
    lo = (a - hi.astype(F32)).astype(BF16)
    return hi, lo


def _dot_r01(a, m01):
    hi, lo = _split2(a)
    mb = m01.astype(BF16)
    return _dot(hi, mb) + _dot(lo, mb)


def _dot_l01(m01, a):
    a1 = a.astype(BF16)
    r1 = a - a1.astype(F32)
    a2 = r1.astype(BF16)
    a3 = (r1 - a2.astype(F32)).astype(BF16)
    mb = m01.astype(BF16)
    return _dot(mb, a1) + _dot(mb, a2) + _dot(mb, a3)


def _dot_r01_full(a, m01):
    a1 = a.astype(BF16)
    r1 = a - a1.astype(F32)
    a2 = r1.astype(BF16)
    a3 = (r1 - a2.astype(F32)).astype(BF16)
    mb = m01.astype(BF16)
    return _dot(a1, mb) + _dot(a2, mb) + _dot(a3, mb)


def _dot3(a, b):
    ah, al = _split2(a)
    bh, bl = _split2(b)
    return _dot(ah, bh) + _dot(al, bh) + _dot(ah, bl)


def _iota(shape, axis):
    return lax.broadcasted_iota(jnp.int32, shape, axis)


def _bd_mask():
    return (_iota((C_MIX, C_MIX), 0) >> 6) == (_iota((C_MIX, C_MIX), 1) >> 6)


def _bdstack(x, bd):
    return jnp.where(bd, jnp.concatenate([x, x, x, x], axis=0), 0.0)


def _fold(f):
    lane_h = _iota((CHUNK, C_MIX), 1) >> 6
    out = jnp.where(lane_h == 0, f[0:CHUNK], 0.0)
    for h in range(1, N_HEADS):
        out = jnp.where(lane_h == h, f[h * CHUNK:(h + 1) * CHUNK], out)
    return out


def _expand_cols(g, first):
    lane_h = _iota((g.shape[0], C_MIX), 1) >> 6
    out = jnp.zeros((g.shape[0], C_MIX), F32)
    for h in range(N_HEADS):
        out = jnp.where(lane_h == h, g[:, first + h:first + h + 1], out)
    return out


def _conv3(x_ref, prev_ref, next_ref, w_ref, first, last):
    xt = x_ref[...]
    n = xt.shape[0]
    rows = _iota((n, 1), 0)
    prev = prev_ref[HALO - 1:HALO, :] * jnp.where(first, 0.0, 1.0)
    nxt = next_ref[0:1, :] * jnp.where(last, 0.0, 1.0)
    up = jnp.where(rows == 0, prev, pltpu.roll(xt, 1, 0))
    dn = jnp.where(rows == n - 1, nxt, pltpu.roll(xt, n - 1, 0))
    return up * w_ref[0:1, :] + xt * w_ref[1:2, :] + dn * w_ref[2:3, :]


def _rms(x, g):
    return x * lax.rsqrt(jnp.mean(x * x, axis=-1, keepdims=True) + RMS_EPS) * g


def _head_stats_norm(y, bdm, eps):
    mu = _dot_r01(y, bdm)
    yc = y - mu
    var = _dot_r01(yc * yc, bdm)
    return yc * lax.rsqrt(var + eps)


def _mods_kernel(c_ref, w_ref, b_ref, o_ref):
    c = c_ref[...]
    s = c * jax.nn.sigmoid(c)
    o_ref[...] = _dot(s, w_ref[...]) + b_ref[...]


def _mods_call(cc, ada_w, ada_b):
    depth, d, n = ada_w.shape
    rows = cc.shape[0]
    tn = 1536
    return pl.pallas_call(
        _mods_kernel,
        grid=(depth, n // tn),
        in_specs=[pl.BlockSpec((rows, d), lambda l, j: (0, 0)),
                  pl.BlockSpec((None, d, tn), lambda l, j: (l, 0, j)),
                  pl.BlockSpec((None, 1, tn), lambda l, j: (l, 0, j))],
        out_specs=pl.BlockSpec((None, rows, tn), lambda l, j: (l, 0, j)),
        out_shape=jax.ShapeDtypeStruct((depth, rows, n), F32),
        compiler_params=pltpu.CompilerParams(vmem_limit_bytes=VMEM_LIMIT),
        name="ada_mods",
    )(cc, ada_w, ada_b.reshape(depth, 1, n))


def _inproj_kernel(t_ctx, x_ref, mc_ref, ml_ref, g_ref, cos_ref, sin_ref,
                   w_rkv, w_z, w_qkv, w_qk, w_vo, w_gt, w_gtt,
                   o_rkv, o_z, o_qkv, o_qk, o_vo, o_gt, o_gtt):
    tm = x_ref.shape[0]
    row = pl.program_id(1) * tm + _iota((tm, 1), 0)
    is_ctx = row < t_ctx
    shift = jnp.where(is_ctx, mc_ref[0:1, :], ml_ref[0:1, :])
    scale = jnp.where(is_ctx, mc_ref[1:2, :], ml_ref[1:2, :])
    h = _rms(x_ref[...], g_ref[...]) * (1.0 + scale) + shift
    hb = h.astype(BF16)
    o_rkv[...] = _dot(hb, w_rkv[...])
    o_z[...] = _dot(hb, w_z[...])
    o_qk[...] = _dot(hb, w_qk[...])
    o_vo[...] = _dot(hb, w_vo[...])
    o_gt[...] = _dot(hb, w_gt[...])
    o_gtt[...] = _dot_nt(w_gtt[...], hb)
    qkv = _dot(hb, w_qkv[...])
    cos = cos_ref[...]
    sin = sin_ref[...]
    first_half = (_iota((tm, 128), 1) & 63) < 32
    n_rot = (C_A + C_KV) // 128
    for i in range(n_rot):
        blk = qkv[:, i * 128:(i + 1) * 128]
        partner = jnp.where(first_half, pltpu.roll(blk, 96, 1), pltpu.roll(blk, 32, 1))
        o_qkv[:, i * 128:(i + 1) * 128] = blk * cos + partner * sin
    o_qkv[:, n_rot * 128:] = qkv[:, n_rot * 128:]


def _inproj_call(x, modc, modl, g, cos_t, sin_t, ws, t_ctx, tm):
    b, l, d = x.shape
    widths = [w.shape[1] for w in ws[:6]]
    row_spec = lambda n: pl.BlockSpec((None, tm, n), lambda i, t: (i, t, 0))
    const = lambda a: pl.BlockSpec(a.shape, lambda i, t: (0,) * a.ndim)
    out_shape = [jax.ShapeDtypeStruct((b, l, n), F32) for n in widths]
    out_shape.append(jax.ShapeDtypeStruct((b, widths[5], l), F32))
    out_specs = [row_spec(n) for n in widths]
    out_specs.append(pl.BlockSpec((None, widths[5], tm), lambda i, t: (i, 0, t)))
    return pl.pallas_call(
        functools.partial(_inproj_kernel, t_ctx),
        grid=(b, l // tm),
        in_specs=[row_spec(d), const(modc),
                  pl.BlockSpec((None, 2, d), lambda i, t: (i, 0, 0)),
                  const(g),
                  pl.BlockSpec((tm, 128), lambda i, t: (t, 0)),
                  pl.BlockSpec((tm, 128), lambda i, t: (t, 0))] + [const(w) for w in ws],
        out_specs=out_specs,
        out_shape=out_shape,
        compiler_params=pltpu.CompilerParams(
            dimension_semantics=("arbitrary", "arbitrary"), vmem_limit_bytes=VMEM_LIMIT),
        name="inproj",
    )(x, modc, modl, g, cos_t, sin_t, *ws)


def _chunk_of(j, d, ncc, nc):
    if d == 0:
        return j
    return jnp.where(j < ncc, ncc - 1 - j, nc + ncc - 1 - j)


def _chunk_specs(width, col, d, ncc, nc):
    per = CHUNK // HALO
    last_halo = nc * per - 1
    chunk = pl.BlockSpec((None, CHUNK, width), lambda b, j: (b, _chunk_of(j, d, ncc, nc), col))
    prev = pl.BlockSpec((None, HALO, width),
                        lambda b, j: (b, jnp.maximum(_chunk_of(j, d, ncc, nc) * per - 1, 0), col))
    nxt = pl.BlockSpec((None, HALO, width),
                       lambda b, j: (b, jnp.minimum(_chunk_of(j, d, ncc, nc) * per + per, last_halo), col))
    return chunk, prev, nxt


def _stream_ends(c, ncc, nc):
    first = jnp.logical_or(c == 0, c == ncc)
    last = jnp.logical_or(c == ncc - 1, c == nc - 1)
    return first, last


def _tri_masks(d):
    t = _iota((CHUNK, C_MIX), 0)
    s = _iota((CHUNK, C_MIX), 1) & (CHUNK - 1)
    t2 = _iota((CHUNK, CHUNK), 0)
    s2 = _iota((CHUNK, CHUNK), 1)
    if d == 0:
        incl, strict, tri = s <= t, s < t, s2 <= t2
    else:
        incl, strict, tri = s >= t, s > t, s2 >= t2
    eye = s == t
    blk16 = (s >> 4) == (t >> 4)
    return incl, strict, eye, blk16, jnp.where(tri, 1.0, 0.0)


def _rwkv_kernel(ncc, nc, *refs):
    (x0, p0, n0, z0, x1, p1, n1, z1, conv_ref, vec_ref, wup_ref, aup_ref, gup_ref,
     y0_ref, y1_ref, bonus_ref, gate_ref, s_ref) = refs
    j = pl.program_id(1)

    @pl.when(j == 0)
    def _():
        s_ref[...] = jnp.zeros_like(s_ref)

    bd = _bd_mask()
    bd_ones = jnp.where(bd, 1.0, 0.0)
    k_k = vec_ref[0:1, :]
    k_a = vec_ref[1:2, :]
    r_k = vec_ref[2:3, :]
    ins = ((x0, p0, n0, z0, y0_ref), (x1, p1, n1, z1, y1_ref))
    for d in range(2):
        x_ref, p_ref, n_ref, z_ref, y_ref = ins[d]
        c_idx = _chunk_of(j, d, ncc, nc)
        first, last = _stream_ends(c_idx, ncc, nc)
        cv = _conv3(x_ref, p_ref, n_ref, conv_ref, first, last)
        r = cv[:, 0:C_MIX]
        k = cv[:, C_MIX:2 * C_MIX]
        v = cv[:, 2 * C_MIX:3 * C_MIX]
        kk = k * k_k
        kk = kk * lax.rsqrt(_dot_r01(kk * kk, bd_ones) + 1e-12)
        zt = z_ref[...]
        zw = zt[:, 0:2 * R_LORA]
        za = zt[:, 2 * R_LORA:4 * R_LORA]
        dw = vec_ref[3 + d:4 + d, :] + _dot(jnp.tanh(zw), wup_ref[d])
        lw = -DECAY_SCALE * jax.nn.sigmoid(dw)
        a = jax.nn.sigmoid(vec_ref[5 + d:6 + d, :] + _dot(za, aup_ref[d]))
        kd = k * (1.0 + (a - 1.0) * k_a)
        bb = kk * a
        if d == 0:
            zg = zt[:, 4 * R_LORA:4 * R_LORA + R_GATE]
            gate_ref[...] = _dot(jax.nn.sigmoid(zg), gup_ref[...])
            bonus_ref[...] = _dot_r01(r * k * r_k, bd_ones) * v

        incl, strict, eye, blk16, tri = _tri_masks(d)
        eye_f = jnp.where(eye, 1.0, 0.0)
        c = _dot_l01(tri, lw)
        ctot = jnp.sum(lw, axis=0, keepdims=True)
        rt = r * jnp.exp(c)
        at = kk * jnp.exp(c - lw)
        e_neg = jnp.exp(-c)
        kh = kd * e_neg
        bh = bb * e_neg
        e_end = jnp.exp(ctot - c)
        kbar = kd * e_end
        bbar = bb * e_end

        lhs = jnp.concatenate([rt, at], axis=0)
        a_k = _dot_nt(lhs, _bdstack(kh, bd))
        a_b = _dot_nt(lhs, _bdstack(bh, bd))
        a_rk = jnp.where(incl, a_k[0:CHUNK], 0.0)
        a_rb = jnp.where(incl, a_b[0:CHUNK], 0.0)
        a_ak = jnp.where(strict, a_k[CHUNK:], 0.0)
        nmat = jnp.where(strict, a_b[CHUNK:], 0.0)

        hmm = lambda p, q: _dot(p, _bdstack(q, bd))
        nd = jnp.where(blk16, nmat, 0.0)
        no = nmat - nd
        p2 = hmm(nd, nd)
        xd = hmm(eye_f - nd, eye_f + p2)
        p4 = hmm(p2, p2)
        xd = hmm(xd, eye_f + p4)
        p8 = hmm(p4, p4)
        xd = hmm(xd, eye_f + p8)
        pm = hmm(xd, no)
        tinv = hmm(hmm(eye_f - pm, eye_f + hmm(pm, pm)), xd)

        v_st = _bdstack(v, bd)
        w1 = hmm(tinv, at)
        u = _dot(hmm(tinv, a_ak), v_st)
        qp = rt - hmm(a_rb, w1)
        y_loc = _dot(a_rk, v_st) - hmm(a_rb, u)
        m_cat = eye_f * jnp.exp(ctot) - _fold(_dot_tn(bbar, w1))
        g_cat = _fold(_dot_tn(kbar, v) - _dot_tn(bbar, u))

        s_st = _bdstack(s_ref[d], bd)
        upd = _dot3(jnp.concatenate([qp, m_cat], axis=0), s_st)
        y_ref[...] = upd[0:CHUNK] + y_loc
        s_ref[d] = upd[CHUNK:] + g_cat


def _rwkv_call(rkv, z, conv, vecs, wup, aup, gup, t_ctx):
    b, l, _ = rkv.shape
    nc, ncc = l // CHUNK, t_ctx // CHUNK
    in_specs = []
    for d in range(2):
        in_specs += list(_chunk_specs(3 * C_MIX, 0, d, ncc, nc))
        in_specs.append(pl.BlockSpec((None, CHUNK, z.shape[2]),
                                     lambda i, j, d=d: (i, _chunk_of(j, d, ncc, nc), 0)))
    const = lambda a: pl.BlockSpec(a.shape, lambda i, j: (0,) * a.ndim)
    in_specs += [const(conv), const(vecs), const(wup), const(aup), const(gup)]
    out_d = lambda d: pl.BlockSpec((None, CHUNK, C_MIX), lambda i, j: (i, _chunk_of(j, d, ncc, nc), 0))
    shp = jax.ShapeDtypeStruct((b, l, C_MIX), F32)
    return pl.pallas_call(
        functools.partial(_rwkv_kernel, ncc, nc),
        grid=(b, nc),
        in_specs=in_specs,
        out_specs=[out_d(0), out_d(1), out_d(0), out_d(0)],
        out_shape=[shp, shp, shp, shp],
        scratch_shapes=[pltpu.VMEM((2, CHUNK, C_MIX), F32)],
        compiler_params=pltpu.CompilerParams(
            dimension_semantics=("arbitrary", "arbitrary"), vmem_limit_bytes=VMEM_LIMIT),
        name="rwkv7_chunked",
    )(rkv, rkv, rkv, z, rkv, rkv, rkv, z, conv, vecs, wup, aup, gup)


def _row_cumsum_matrix(d, bd):
    src = _iota((C_MIX, C_MIX), 0) & (CHUNK - 1)
    dst = _iota((C_MIX, C_MIX), 1) & (CHUNK - 1)
    keep = (src <= dst) if d == 0 else (src >= dst)
    return jnp.where(bd & keep, 1.0, 0.0)


def _log_sigmoid(x):
    return jnp.minimum(x, 0.0) - jnp.log1p(jnp.exp(-jnp.abs(x)))


def _mlstm_kernel(ncc, nc, *refs):
    (x0, p0, n0, v0, g0, gr0, x1, p1, n1, v1, g1, gr1, conv_ref, bcol_ref, brow_ref,
     h0_ref, h1_ref, c_ref, n_ref, m_ref) = refs
    j = pl.program_id(1)

    @pl.when(j == 0)
    def _():
        c_ref[...] = jnp.zeros_like(c_ref)
        n_ref[...] = jnp.zeros_like(n_ref)
        m_ref[...] = jnp.zeros_like(m_ref)

    bd = _bd_mask()
    bd_ones = jnp.where(bd, 1.0, 0.0)
    lane_h = _iota((CHUNK, C_MIX), 1) >> 6
    ins = ((x0, p0, n0, v0, g0, gr0, h0_ref), (x1, p1, n1, v1, g1, gr1, h1_ref))
    for d in range(2):
        x_ref, p_ref, nx_ref, v_ref, g_ref, gr_ref, h_ref = ins[d]
        c_idx = _chunk_of(j, d, ncc, nc)
        first, last = _stream_ends(c_idx, ncc, nc)
        cv = _conv3(x_ref, p_ref, nx_ref, conv_ref, first, last)
        qk = cv * jax.nn.sigmoid(cv)
        q = qk[:, 0:C_MIX]
        k = qk[:, C_MIX:] * (HEAD_DIM ** -0.5)
        v = v_ref[...]

        incl, _, _, _, tri = _tri_masks(d)
        gcol = g_ref[...] + bcol_ref[...]
        i_exp = _expand_cols(gcol, 4 * d)
        f_exp = _log_sigmoid(_expand_cols(gcol, 8 + 4 * d))
        grow = gr_ref[...] + brow_ref[...]
        i_row = grow[d:d + 1, :]
        f_row = _log_sigmoid(grow[2 + d:3 + d, :])
        b_exp = _dot_l01(tri, f_exp)
        b_row = _dot_r01_full(jnp.broadcast_to(f_row, (HALO, C_MIX)), _row_cumsum_matrix(d, bd))[0:1, :]
        bend = jnp.sum(f_exp, axis=0, keepdims=True)

        logd = jnp.where(incl, b_exp - b_row + i_row, NEG_INF)
        mloc = jnp.zeros((CHUNK, C_MIX), F32)
        for h in range(N_HEADS):
            mx = jnp.max(jnp.where(lane_h == h, logd, NEG_INF), axis=-1, keepdims=True)
            mloc = jnp.where(lane_h == h, mx, mloc)
        sw = _dot_nt(q, _bdstack(k, bd)) * jnp.exp(logd - mloc)
        v_st = _bdstack(v, bd)
        num_loc = _dot(sw, v_st)
        den_loc = _dot(sw, bd_ones)

        g_col = bend - b_exp + i_exp
        mg = jnp.max(g_col, axis=0, keepdims=True)
        kw = k * jnp.exp(g_col - mg)
        kg_cat = _fold(_dot_tn(kw, v))
        kn = jnp.sum(kw, axis=0, keepdims=True)

        m_prev = m_ref[d][0:1, :]
        n_prev = n_ref[d][0:1, :]
        c_prev = c_ref[d]
        inter = b_exp + m_prev
        m_t = jnp.maximum(inter, mloc)
        a_loc = jnp.exp(mloc - m_t)
        w_int = jnp.exp(inter - m_t)
        num = a_loc * num_loc + w_int * _dot(q, _bdstack(c_prev, bd))
        den = a_loc * den_loc + w_int * _dot(q * n_prev, bd_ones)
        h_ref[...] = num / jnp.maximum(jnp.abs(den), jnp.exp(-m_t))

        m_new = jnp.maximum(bend + m_prev, mg)
        sc = jnp.exp(mg - m_new)
        dec = jnp.exp(bend + m_prev - m_new)
        c_ref[d] = dec * c_prev + sc * kg_cat
        n_ref[d] = jnp.broadcast_to(dec * n_prev + sc * kn, (HALO, C_MIX))
        m_ref[d] = jnp.broadcast_to(m_new, (HALO, C_MIX))


def _mlstm_call(qk, vo, gt, grow, conv, bcol, brow, t_ctx):
    b, l, _ = qk.shape
    nc, ncc = l // CHUNK, t_ctx // CHUNK
    in_specs = []
    for d in range(2):
        cidx = lambda i, j, d=d: (i, _chunk_of(j, d, ncc, nc), 0)
        in_specs += list(_chunk_specs(2 * C_MIX, 0, d, ncc, nc))
        in_specs.append(pl.BlockSpec((None, CHUNK, C_MIX), cidx))
        in_specs.append(pl.BlockSpec((None, CHUNK, gt.shape[2]), cidx))
        in_specs.append(pl.BlockSpec((None, None, 4, C_MIX),
                                     lambda i, j, d=d: (i, _chunk_of(j, d, ncc, nc), 0, 0)))
    const = lambda a: pl.BlockSpec(a.shape, lambda i, j: (0,) * a.ndim)
    in_specs += [const(conv), const(bcol), const(brow)]
    out_d = lambda d: pl.BlockSpec((None, CHUNK, C_MIX), lambda i, j: (i, _chunk_of(j, d, ncc, nc), 0))
    shp = jax.ShapeDtypeStruct((b, l, C_MIX), F32)
    return pl.pallas_call(
        functools.partial(_mlstm_kernel, ncc, nc),
        grid=(b, nc),
        in_specs=in_specs,
        out_specs=[out_d(0), out_d(1)],
        out_shape=[shp, shp],
        scratch_shapes=[pltpu.VMEM((2, CHUNK, C_MIX), F32),
                        pltpu.VMEM((2, HALO, C_MIX), F32),
                        pltpu.VMEM((2, HALO, C_MIX), F32)],
        compiler_params=pltpu.CompilerParams(
            dimension_semantics=("arbitrary", "arbitrary"), vmem_limit_bytes=VMEM_LIMIT),
        name="mlstm_chunked",
    )(qk, qk, qk, vo, gt, grow, qk, qk, qk, vo, gt, grow, conv, bcol, brow)


def _attn_kernel(t_ctx, qb0, n_lat, sink_ref, q_ref, kc_ref, vc_ref, ka_ref, kb_ref, kd_ref,
                 va_ref, vb_ref, vd_ref, o_ref):
    h = pl.program_id(1)
    qb = pl.program_id(2) + qb0
    ncb = t_ctx // ATTN_BLOCK
    lb = qb - ncb
    bq = ATTN_BLOCK
    kmat = jnp.concatenate([kc_ref[...], ka_ref[...], kb_ref[...], kd_ref[...]], axis=0)
    vmat = jnp.concatenate([vc_ref[...], va_ref[...], vb_ref[...], vd_ref[...]], axis=0)
    nk = kmat.shape[0]
    own = (_iota((nk, 2 * HEAD_DIM), 1) >> 6) == h
    k2 = jnp.where(own, kmat, pltpu.roll(kmat, HEAD_DIM, 1))
    v2 = jnp.where(own, vmat, pltpu.roll(vmat, HEAD_DIM, 1))

    q_pos = _iota((bq, nk), 0) + bq
    col = _iota((bq, nk), 1)
    k_rel = col - t_ctx
    k_abs = k_rel + (lb - 1) * bq
    valid_loc = (jnp.abs(k_rel - q_pos) <= WINDOW) & (k_abs >= 0) & (k_abs < n_lat) & (lb >= 0)
    valid = (col < t_ctx) | valid_loc

    lane_lo = _iota((bq, 2 * HEAD_DIM), 1) < HEAD_DIM
    scale = HEAD_DIM ** -0.5
    for p in range(ATTN_GROUP // 2):
        qp = q_ref[:, p * 128:(p + 1) * 128] * scale
        outs = []
        for half in range(2):
            qm = jnp.where(lane_lo, qp, 0.0) if half == 0 else jnp.where(lane_lo, 0.0, qp)
            s = jnp.where(valid, _dot_nt(qm, k2), NEG_INF)
            sk = sink_ref[h * ATTN_GROUP + 2 * p + half]
            m = jnp.maximum(jnp.max(s, axis=-1, keepdims=True), sk)
            e = jnp.exp(s - m)
            den = jnp.sum(e, axis=-1, keepdims=True) + jnp.exp(sk - m)
            outs.append(_dot(e, v2) / den)
        o_ref[:, p * 128:(p + 1) * 128] = jnp.where(lane_lo, outs[0], outs[1])


def _attn_call(qkv, sink, t_ctx, with_ctx_queries):
    b, l, _ = qkv.shape
    nqb = l // ATTN_BLOCK
    ncb = t_ctx // ATTN_BLOCK
    qb0 = 0 if with_ctx_queries else ncb
    kcol, vcol = C_A // 128, C_A // 128 + 1
    qw = ATTN_GROUP * HEAD_DIM

    def loc(off, col):
        return pl.BlockSpec((None, ATTN_BLOCK, 128),
                            lambda i, h, t: (i, jnp.clip(t + qb0 + off, ncb, nqb - 1), col))

    return pl.pallas_call(
        functools.partial(_attn_kernel, t_ctx, qb0, l - t_ctx),
        grid=(b, ATTN_KV_HEADS, nqb - qb0),
        in_specs=[pl.BlockSpec(memory_space=pltpu.SMEM),
                  pl.BlockSpec((None, ATTN_BLOCK, qw), lambda i, h, t: (i, t + qb0, h)),
                  pl.BlockSpec((None, t_ctx, 128), lambda i, h, t: (i, 0, kcol)),
                  pl.BlockSpec((None, t_ctx, 128), lambda i, h, t: (i, 0, vcol)),
                  loc(-1, kcol), loc(0, kcol), loc(1, kcol),
                  loc(-1, vcol), loc(0, vcol), loc(1, vcol)],
        out_specs=pl.BlockSpec((None, ATTN_BLOCK, qw), lambda i, h, t: (i, t + qb0, h)),
        out_shape=jax.ShapeDtypeStruct((b, l, C_A), F32),
        compiler_params=pltpu.CompilerParams(
            dimension_semantics=("arbitrary", "arbitrary", "arbitrary"), vmem_limit_bytes=VMEM_LIMIT),
        name="attn_window",
    )(sink, qkv, qkv, qkv, qkv, qkv, qkv, qkv, qkv, qkv)


def _outmlp_kernel(t_ctx, row0, final, x_ref, y0_ref, y1_ref, bonus_ref, gate_ref, ya_ref,
                   h0_ref, h1_ref, og_ref, mc_ref, ml_ref, g2_ref, lnv_ref, fin_ref,
                   wo_ref, w1_ref, w2_ref, o_ref):
    tm = x_ref.shape[0]
    row = row0 + pl.program_id(1) * tm + _iota((tm, 1), 0)
    is_ctx = row < t_ctx
    mod = lambda i: jnp.where(is_ctx, mc_ref[i:i + 1, :], ml_ref[i:i + 1, :])
    bdm = jnp.where(_bd_mask(), 1.0 / HEAD_DIM, 0.0)

    yr = _head_stats_norm(y0_ref[...] + y1_ref[...], bdm, RWKV_LN_EPS) * lnv_ref[0:1, :] + lnv_ref[1:2, :]
    yr = (yr + bonus_ref[...]) * gate_ref[...]
    ym = _head_stats_norm(h0_ref[...] + h1_ref[...], bdm, MLSTM_LN_EPS) * lnv_ref[2:3, :]
    ym = jax.nn.sigmoid(og_ref[...]) * ym
    mix = (_dot(yr.astype(BF16), wo_ref[0:C_MIX, :])
           + _dot(ya_ref[...].astype(BF16), wo_ref[C_MIX:C_MIX + C_A, :])
           + _dot(ym.astype(BF16), wo_ref[C_MIX + C_A:, :]))
    x1 = x_ref[...] + mod(0) * mix
    h2 = (_rms(x1, g2_ref[...]) * (1.0 + mod(2)) + mod(1)).astype(BF16)
    d_ff = w1_ref.shape[1]
    step = 1024
    acc = jnp.zeros_like(x1)
    for c0 in range(0, d_ff, step):
        hid = jnp.maximum(_dot(h2, w1_ref[:, c0:c0 + step]), 0.0)
        acc = acc + _dot((hid * hid).astype(BF16), w2_ref[c0:c0 + step, :])
    x2 = x1 + mod(3) * acc
    if final:
        x2 = _rms(x2, fin_ref[...])
    o_ref[...] = x2


def _outmlp_call(x, y0, y1, bonus, gate, ya, h0, h1, vo, modc, modl, g2, lnv, fin, wo, w1, w2,
                 t_ctx, tm, final):
    b, l, d = x.shape
    t0 = t_ctx // tm if final else 0
    n_t = l // tm - t0
    row = lambda n, col=0: pl.BlockSpec((None, tm, n), lambda i, t: (i, t + t0, col))
    const = lambda a: pl.BlockSpec(a.shape, lambda i, t: (0,) * a.ndim, pipeline_mode=pl.Buffered(1))
    small = lambda a: pl.BlockSpec(a.shape, lambda i, t: (0,) * a.ndim)
    out_rows = l - t0 * tm
    return pl.pallas_call(
        functools.partial(_outmlp_kernel, t_ctx, t0 * tm, final),
        grid=(b, n_t),
        in_specs=[row(d), row(C_MIX), row(C_MIX), row(C_MIX), row(C_MIX), row(C_A),
                  row(C_MIX), row(C_MIX), row(C_MIX, 1),
                  small(modc), pl.BlockSpec((None, 4, d), lambda i, t: (i, 0, 0)),
                  small(g2), small(lnv), small(fin), const(wo), const(w1), const(w2)],
        out_specs=pl.BlockSpec((None, tm, d), lambda i, t: (i, t, 0)),
        out_shape=jax.ShapeDtypeStruct((b, out_rows, d), F32),
        compiler_params=pltpu.CompilerParams(
            dimension_semantics=("arbitrary", "arbitrary"), vmem_limit_bytes=VMEM_LIMIT),
        name="outproj_mlp",
    )(x, y0, y1, bonus, gate, ya, h0, h1, vo, modc, modl, g2, lnv, fin, wo, w1, w2)


def _rope_tables(s, t_ctx):
    rows = s // GRID_W
    n_freq = HEAD_DIM // 4
    row = jnp.repeat(jnp.arange(rows), GRID_W).astype(F32)
    col = jnp.tile(jnp.arange(GRID_W), rows).astype(F32)
    inv = ROPE_BASE ** (-jnp.arange(n_freq, dtype=F32) / n_freq)
    ang = jnp.concatenate([row[:, None] * inv, col[:, None] * inv], axis=-1)
    cos, sin = jnp.cos(ang), jnp.sin(ang)
    cos_l = jnp.concatenate([cos, cos, cos, cos], axis=-1)
    sin_l = jnp.concatenate([-sin, sin, -sin, sin], axis=-1)
    cos_t = jnp.concatenate([jnp.ones((t_ctx, 128), F32), cos_l], axis=0)
    sin_t = jnp.concatenate([jnp.zeros((t_ctx, 128), F32), sin_l], axis=0)
    return cos_t, sin_t


def _pad_dir_lora(w):
    r = w.shape[1]
    z = jnp.zeros_like(w[0])
    return jnp.stack([jnp.concatenate([w[0], z], axis=0), jnp.concatenate([z, w[1]], axis=0)], axis=0)


def kernel(x, c, ctx, c_ctx, ada_w, ada_b, norm1_g, norm2_g, w_in, w_out, rwkv_conv, rwkv_w0, rwkv_w_up, rwkv_a0, rwkv_a_up, rwkv_g_up, rwkv_k_k, rwkv_k_a, rwkv_r_k, rwkv_ln_w, rwkv_ln_b, attn_sink, mlstm_conv, mlstm_b_i, mlstm_b_f, mlstm_norm_g, mlp_w1, mlp_w2, final_g):
    b, s, d = x.shape
    t_ctx = ctx.shape[1]
    depth = ada_w.shape[0]
    l = t_ctx + s
    nc = l // CHUNK
    assert s % ATTN_BLOCK == 0 and t_ctx % ATTN_BLOCK == 0 and t_ctx % CHUNK == 0 and s % GRID_W == 0
    tm = 256
    assert l % tm == 0 and t_ctx % tm == 0

    n_mod_rows = -(-(b + 1) // HALO) * HALO
    cc = jnp.zeros((n_mod_rows, d), F32).at[:b].set(c).at[b].set(c_ctx)
    mods = _mods_call(cc, ada_w, ada_b).reshape(depth, n_mod_rows, 6, d)
    cos_t, sin_t = _rope_tables(s, t_ctx)
    xc = jnp.concatenate([ctx, x], axis=1)

    splits = [0, 3 * C_MIX]
    for w_ in (4 * R_LORA + R_GATE, C_A + 2 * C_KV, 2 * C_MIX, 2 * C_MIX, 4 * N_HEADS):
        splits.append(splits[-1] + w_)

    for li in range(depth):
        last = li == depth - 1
        m_l = mods[li, :b]
        m_c = mods[li, b]
        wl = w_in[li].astype(BF16)
        ws = [wl[:, splits[i]:splits[i + 1]] for i in range(6)]
        ws.append(jnp.transpose(ws[5]))
        rkv, z, qkv, qk, vo, gt, gtt = _inproj_call(
            xc, m_c[0:2], m_l[:, 0:2], norm1_g[li].reshape(1, d), cos_t, sin_t, ws, t_ctx, tm)

        vecs = jnp.stack([rwkv_k_k[li], rwkv_k_a[li], rwkv_r_k[li], rwkv_w0[li, 0], rwkv_w0[li, 1],
                          rwkv_a0[li, 0], rwkv_a0[li, 1], jnp.zeros((C_MIX,), F32)], axis=0)
        y0, y1, bonus, gate = _rwkv_call(rkv, z, rwkv_conv[li], vecs, _pad_dir_lora(rwkv_w_up[li]),
                                         _pad_dir_lora(rwkv_a_up[li]), rwkv_g_up[li], t_ctx)

        ya = _attn_call(qkv, attn_sink[li], t_ctx, not last)

        grow = gtt.reshape(b, 4, N_HEADS, nc, CHUNK).transpose(0, 3, 1, 2, 4).reshape(b, nc, 4, C_MIX)
        bias = jnp.concatenate([mlstm_b_i[li].reshape(-1), mlstm_b_f[li].reshape(-1)])
        bcol = bias.reshape(1, 4 * N_HEADS)
        brow = jnp.repeat(bias.reshape(4, N_HEADS), CHUNK, axis=1)
        h0, h1 = _mlstm_call(qk, vo, gt, grow, mlstm_conv[li], bcol, brow, t_ctx)

        lnv = jnp.stack([rwkv_ln_w[li], rwkv_ln_b[li], mlstm_norm_g[li], jnp.zeros((C_MIX,), F32)], axis=0)
        lnv = jnp.concatenate([lnv, jnp.zeros((4, C_MIX), F32)], axis=0)
        xc = _outmlp_call(xc, y0, y1, bonus, gate, ya, h0, h1, vo, m_c[2:6], m_l[:, 2:6],
                          norm2_g[li].reshape(1, d), lnv, final_g.reshape(1, d),
                          w_out[li].astype(BF16), mlp_w1[li].astype(BF16), mlp_w2[li].astype(BF16),
                          t_ctx, tm, last)
    return xc
```

```python
import functools

import jax
import jax.numpy as jnp
from jax import lax
from jax.experimental import pallas as pl
from jax.experimental.pallas import tpu as pltpu

F32 = jnp.float32
BF16 = jnp.bfloat16

HEAD_DIM = 64
N_HEADS = 4
C_MIX = N_HEADS * HEAD_DIM
ATTN_KV_HEADS = 2
ATTN_GROUP = 4
C_A = ATTN_KV_HEADS * ATTN_GROUP * HEAD_DIM
C_KV = ATTN_KV_HEADS * HEAD_DIM
GRID_W = 64
ATTN_BLOCK = 128
WINDOW = 128
ROPE_BASE = 10000.0
R_LORA = 64
R_GATE = 128
DECAY_SCALE = 0.606531
RMS_EPS = 1e-6
RWKV_LN_EPS = 64e-5
MLSTM_LN_EPS = 1e-6
NEG_INF = -1e30
CHUNK = 64
GROUP = 4
BLOCK = GROUP * CHUNK
HALO = 8
VMEM_LIMIT = 56 * 1024 * 1024


def _dot(a, b):
    return jnp.dot(a, b, preferred_element_type=F32)


def _dot_nt(a, b):
    return lax.dot_general(a, b, (((1,), (1,)), ((), ())), preferred_element_type=F32)


def _dot_tn(a, b):
    return lax.dot_general(a, b, (((0,), (0,)), ((), ())), preferred_element_type=F32)


def _split2(a):
    hi = a.astype(BF16)
    lo = (a - hi.astype(F32)).astype(BF16)
    return hi, lo


def _split3(a):
    a1 = a.astype(BF16)
    r1 = a - a1.astype(F32)
    a2 = r1.astype(BF16)
    a3 = (r1 - a2.astype(F32)).astype(BF16)
    return a1, a2, a3


def _dot_r01(a, m01):
    hi, lo = _split2(a)
    mb = m01.astype(BF16)
    return _dot(hi, mb) + _dot(lo, mb)


def _dot_l01(m01, a):
    a1, a2, a3 = _split3(a)
    mb = m01.astype(BF16)
    return _dot(mb, a1) + _dot(mb, a2) + _dot(mb, a3)


def _dot_r01_full(a, m01):
    a1, a2, a3 = _split3(a)
    mb = m01.astype(BF16)
    return _dot(a1, mb) + _dot(a2, mb) + _dot(a3, mb)


def _dot3(a, b):
    ah, al = _split2(a)
    bh, bl = _split2(b)
    return _dot(ah, bh) + _dot(al, bh) + _dot(ah, bl)


def _iota(shape, axis):
    return lax.broadcasted_iota(jnp.int32, shape, axis)


def _bd_mask():
    return (_iota((C_MIX, C_MIX), 0) >> 6) == (_iota((C_MIX, C_MIX), 1) >> 6)


def _bdstack(x, bd):
    return jnp.where(bd, jnp.concatenate([x, x, x, x], axis=0), 0.0)


def _fold(f):
    lane_h = _iota((CHUNK, C_MIX), 1) >> 6
    out = jnp.where(lane_h == 0, f[0:CHUNK], 0.0)
    for h in range(1, N_HEADS):
        out = jnp.where(lane_h == h, f[h * CHUNK:(h + 1) * CHUNK], out)
    return out


def _expand_cols(g, first):
    lane_h = _iota((g.shape[0], C_MIX), 1) >> 6
    out = jnp.zeros((g.shape[0], C_MIX), F32)
    for h in range(N_HEADS):
        out = jnp.where(lane_h == h, g[:, first + h:first + h + 1], out)
    return out


def _conv3(x_ref, prev_ref, next_ref, w_ref, first, last):
    xt = x_ref[...]
    n = xt.shape[0]
    rows = _iota((n, 1), 0)
    prev = prev_ref[HALO - 1:HALO, :] * jnp.where(first, 0.0, 1.0)
    nxt = next_ref[0:1, :] * jnp.where(last, 0.0, 1.0)
    up = jnp.where(rows == 0, prev, pltpu.roll(xt, 1, 0))
    dn = jnp.where(rows == n - 1, nxt, pltpu.roll(xt, n - 1, 0))
    return up * w_ref[0:1, :] + xt * w_ref[1:2, :] + dn * w_ref[2:3, :]


def _rms(x, g):
    return x * lax.rsqrt(jnp.mean(x * x, axis=-1, keepdims=True) + RMS_EPS) * g


def _head_stats_norm(y, bdm, eps):
    mu = _dot_r01(y, bdm)
    yc = y - mu
    var = _dot_r01(yc * yc, bdm)
    return yc * lax.rsqrt(var + eps)


def _mods_kernel(c_ref, w_ref, b_ref, o_ref):
    c = c_ref[...]
    s = c * jax.nn.sigmoid(c)
    o_ref[...] = _dot(s, w_ref[...]) + b_ref[...]


def _mods_call(cc, ada_w, ada_b):
    depth, d, n = ada_w.shape
    rows = cc.shape[0]
    tn = 1536
    return pl.pallas_call(
        _mods_kernel,
        grid=(depth, n // tn),
        in_specs=[pl.BlockSpec((rows, d), lambda l, j: (0, 0)),
                  pl.BlockSpec((None, d, tn), lambda l, j: (l, 0, j)),
                  pl.BlockSpec((None, 1, tn), lambda l, j: (l, 0, j))],
        out_specs=pl.BlockSpec((None, rows, tn), lambda l, j: (l, 0, j)),
        out_shape=jax.ShapeDtypeStruct((depth, rows, n), F32),
        compiler_params=pltpu.CompilerParams(vmem_limit_bytes=VMEM_LIMIT),
        name="ada_mods",
    )(cc, ada_w, ada_b.reshape(depth, 1, n))


def _inproj_kernel(t_ctx, x_ref, mc_ref, ml_ref, g_ref, cos_ref, sin_ref,
                   w_rkv, w_z, w_qkv, w_qk, w_vo, w_gt, w_gtt,
                   o_rkv, o_z, o_qkv, o_qk, o_vo, o_gt, o_gtt):
    tm = x_ref.shape[0]
    row = pl.program_id(1) * tm + _iota((tm, 1), 0)
    is_ctx = row < t_ctx
    shift = jnp.where(is_ctx, mc_ref[0:1, :], ml_ref[0:1, :])
    scale = jnp.where(is_ctx, mc_ref[1:2, :], ml_ref[1:2, :])
    h = _rms(x_ref[...], g_ref[...]) * (1.0 + scale) + shift
    hb = h.astype(BF16)
    o_rkv[...] = _dot(hb, w_rkv[...])
    o_z[...] = _dot(hb, w_z[...])
    o_qk[...] = _dot(hb, w_qk[...])
    o_vo[...] = _dot(hb, w_vo[...])
    o_gt[...] = _dot(hb, w_gt[...])
    o_gtt[...] = _dot_nt(w_gtt[...], hb)
    qkv = _dot(hb, w_qkv[...])
    cos = cos_ref[...]
    sin = sin_ref[...]
    first_half = (_iota((tm, 128), 1) & 63) < 32
    n_rot = (C_A + C_KV) // 128
    for i in range(n_rot):
        blk = qkv[:, i * 128:(i + 1) * 128]
        partner = jnp.where(first_half, pltpu.roll(blk, 96, 1), pltpu.roll(blk, 32, 1))
        o_qkv[:, i * 128:(i + 1) * 128] = blk * cos + partner * sin
    o_qkv[:, n_rot * 128:] = qkv[:, n_rot * 128:]


def _inproj_call(x, modc, modl, g, cos_t, sin_t, ws, t_ctx, tm):
    b, l, d = x.shape
    widths = [w.shape[1] for w in ws[:6]]
    row_spec = lambda n: pl.BlockSpec((None, tm, n), lambda i, t: (i, t, 0))
    const = lambda a: pl.BlockSpec(a.shape, lambda i, t: (0,) * a.ndim)
    out_shape = [jax.ShapeDtypeStruct((b, l, n), F32) for n in widths]
    out_shape.append(jax.ShapeDtypeStruct((b, widths[5], l), F32))
    out_specs = [row_spec(n) for n in widths]
    out_specs.append(pl.BlockSpec((None, widths[5], tm), lambda i, t: (i, 0, t)))
    return pl.pallas_call(
        functools.partial(_inproj_kernel, t_ctx),
        grid=(b, l // tm),
        in_specs=[row_spec(d), const(modc),
                  pl.BlockSpec((None, 2, d), lambda i, t: (i, 0, 0)),
                  const(g),
                  pl.BlockSpec((tm, 128), lambda i, t: (t, 0)),
                  pl.BlockSpec((tm, 128), lambda i, t: (t, 0))] + [const(w) for w in ws],
        out_specs=out_specs,
        out_shape=out_shape,
        compiler_params=pltpu.CompilerParams(
            dimension_semantics=("arbitrary", "arbitrary"), vmem_limit_bytes=VMEM_LIMIT),
        name="inproj",
    )(x, modc, modl, g, cos_t, sin_t, *ws)


def _block_of(j, d, nbc, nb):
    if d == 0:
        return j
    return jnp.where(j < nbc, nbc - 1 - j, nb + nbc - 1 - j)


def _block_specs(width, col, d, nbc, nb):
    per = BLOCK // HALO
    last_halo = nb * per - 1
    blk = pl.BlockSpec((None, BLOCK, width), lambda b, j: (b, _block_of(j, d, nbc, nb), col))
    prev = pl.BlockSpec((None, HALO, width),
                        lambda b, j: (b, jnp.maximum(_block_of(j, d, nbc, nb) * per - 1, 0), col))
    nxt = pl.BlockSpec((None, HALO, width),
                       lambda b, j: (b, jnp.minimum(_block_of(j, d, nbc, nb) * per + per, last_halo), col))
    return blk, prev, nxt


def _stream_ends(blk, nbc, nb):
    first = jnp.logical_or(blk == 0, blk == nbc)
    last = jnp.logical_or(blk == nbc - 1, blk == nb - 1)
    return first, last


def _tri_masks(d):
    t = _iota((CHUNK, C_MIX), 0)
    s = _iota((CHUNK, C_MIX), 1) & (CHUNK - 1)
    if d == 0:
        incl, strict = s <= t, s < t
    else:
        incl, strict = s >= t, s > t
    return incl, strict, s == t, (s >> 4) == (t >> 4)


def _block_cumsum_matrix(d):
    t = _iota((BLOCK, BLOCK), 0)
    s = _iota((BLOCK, BLOCK), 1)
    same = (t >> 6) == (s >> 6)
    keep = (s <= t) if d == 0 else (s >= t)
    return jnp.where(same & keep, 1.0, 0.0)


def _each(f, *lists):
    return [f(*xs) for xs in zip(*lists)]


def _rwkv_terms(bd, items):
    hmm = lambda ps, qs: [_dot(p, _bdstack(q, bd)) for p, q in zip(ps, qs)]
    plus = lambda xs, ys: _each(lambda x, y: x + y, xs, ys)
    minus = lambda xs, ys: _each(lambda x, y: x - y, xs, ys)
    times = lambda xs, ys: _each(lambda x, y: x * y, xs, ys)
    d_, r, kk, kd, bb, v, c, lw = [list(x) for x in zip(*items)]
    masks = [_tri_masks(d) for d in d_]
    incl = [m[0] for m in masks]
    strict = [m[1] for m in masks]
    eye_f = [jnp.where(m[2], 1.0, 0.0) for m in masks]
    blk16 = [m[3] for m in masks]
    ctot = [ci[CHUNK - 1:CHUNK, :] if d == 0 else ci[0:1, :] for d, ci in zip(d_, c)]
    rt = _each(lambda x, ci: x * jnp.exp(ci), r, c)
    at = _each(lambda x, ci, li: x * jnp.exp(ci - li), kk, c, lw)
    e_neg = _each(lambda ci: jnp.exp(-ci), c)
    kh = times(kd, e_neg)
    bh = times(bb, e_neg)
    e_end = _each(lambda ti, ci: jnp.exp(ti - ci), ctot, c)
    kbar = times(kd, e_end)
    bbar = times(bb, e_end)
    lhs = _each(lambda x, y: jnp.concatenate([x, y], axis=0), rt, at)
    a_k = _each(lambda x, y: _dot_nt(x, _bdstack(y, bd)), lhs, kh)
    a_b = _each(lambda x, y: _dot_nt(x, _bdstack(y, bd)), lhs, bh)
    a_rk = _each(lambda m, x: jnp.where(m, x[0:CHUNK], 0.0), incl, a_k)
    a_rb = _each(lambda m, x: jnp.where(m, x[0:CHUNK], 0.0), incl, a_b)
    a_ak = _each(lambda m, x: jnp.where(m, x[CHUNK:], 0.0), strict, a_k)
    nmat = _each(lambda m, x: jnp.where(m, x[CHUNK:], 0.0), strict, a_b)
    nd = _each(lambda m, x: jnp.where(m, x, 0.0), blk16, nmat)
    no = minus(nmat, nd)
    p2 = hmm(nd, nd)
    xd = hmm(minus(eye_f, nd), plus(eye_f, p2))
    p4 = hmm(p2, p2)
    xd = hmm(xd, plus(eye_f, p4))
    p8 = hmm(p4, p4)
    xd = hmm(xd, plus(eye_f, p8))
    pm = hmm(xd, no)
    pp = hmm(pm, pm)
    tinv = hmm(hmm(minus(eye_f, pm), plus(eye_f, pp)), xd)
    v_st = [_bdstack(x, bd) for x in v]
    w1 = hmm(tinv, at)
    ta = hmm(tinv, a_ak)
    u = _each(_dot, ta, v_st)
    qp = minus(rt, hmm(a_rb, w1))
    y_loc = minus(_each(_dot, a_rk, v_st), hmm(a_rb, u))
    m_cat = _each(lambda e, t, x, y: e * jnp.exp(t) - _fold(_dot_tn(x, y)), eye_f, ctot, bbar, w1)
    g_cat = _each(lambda x, y, z, w: _fold(_dot_tn(x, y) - _dot_tn(z, w)), kbar, v, bbar, u)
    return list(zip(qp, y_loc, m_cat, g_cat))


def _rwkv_kernel(nbc, nb, *refs):
    (x0, p0, n0, z0, x1, p1, n1, z1, conv_ref, vec_ref, wup_ref, aup_ref, gup_ref,
     y0_ref, y1_ref, bonus_ref, gate_ref, s_ref) = refs
    j = pl.program_id(1)

    @pl.when(j == 0)
    def _():
        s_ref[...] = jnp.zeros_like(s_ref)

    bd = _bd_mask()
    bd_ones = jnp.where(bd, 1.0, 0.0)
    k_k = vec_ref[0:1, :]
    k_a = vec_ref[1:2, :]
    r_k = vec_ref[2:3, :]
    ins = ((x0, p0, n0, z0), (x1, p1, n1, z1))
    y_refs = (y0_ref, y1_ref)
    items = []
    for d in range(2):
        x_ref, p_ref, n_ref, z_ref = ins[d]
        first, last = _stream_ends(_block_of(j, d, nbc, nb), nbc, nb)
        cv = _conv3(x_ref, p_ref, n_ref, conv_ref, first, last)
        r = cv[:, 0:C_MIX]
        k = cv[:, C_MIX:2 * C_MIX]
        v = cv[:, 2 * C_MIX:3 * C_MIX]
        kk = k * k_k
        kk = kk * lax.rsqrt(_dot_r01(kk * kk, bd_ones) + 1e-12)
        zt = z_ref[...]
        zw = zt[:, 0:2 * R_LORA]
        za = zt[:, 2 * R_LORA:4 * R_LORA]
        dw = vec_ref[3 + d:4 + d, :] + _dot(jnp.tanh(zw), wup_ref[d])
        lw = -DECAY_SCALE * jax.nn.sigmoid(dw)
        a = jax.nn.sigmoid(vec_ref[5 + d:6 + d, :] + _dot(za, aup_ref[d]))
        kd = k * (1.0 + (a - 1.0) * k_a)
        bb = kk * a
        if d == 0:
            zg = zt[:, 4 * R_LORA:4 * R_LORA + R_GATE]
            gate_ref[...] = _dot(jax.nn.sigmoid(zg), gup_ref[...])
            bonus_ref[...] = _dot_r01(r * k * r_k, bd_ones) * v
        c = _dot_l01(_block_cumsum_matrix(d), lw)
        for g in range(GROUP):
            sl = slice(g * CHUNK, (g + 1) * CHUNK)
            items.append((d, r[sl], kk[sl], kd[sl], bb[sl], v[sl], c[sl], lw[sl]))
    terms = _rwkv_terms(bd, items)
    states = [s_ref[0], s_ref[1]]
    for step in range(GROUP):
        chunk = (step, GROUP - 1 - step)
        ups = []
        for d in range(2):
            qp, _, m_cat, _ = terms[d * GROUP + chunk[d]]
            ups.append(_dot3(jnp.concatenate([qp, m_cat], axis=0), _bdstack(states[d], bd)))
        for d in range(2):
            g = chunk[d]
            _, y_loc, _, g_cat = terms[d * GROUP + g]
            y_refs[d][g * CHUNK:(g + 1) * CHUNK, :] = ups[d][0:CHUNK] + y_loc
            states[d] = ups[d][CHUNK:] + g_cat
    s_ref[0] = states[0]
    s_ref[1] = states[1]


def _rwkv_call(rkv, z, conv, vecs, wup, aup, gup, t_ctx):
    b, l, _ = rkv.shape
    nb, nbc = l // BLOCK, t_ctx // BLOCK
    in_specs = []
    for d in range(2):
        in_specs += list(_block_specs(3 * C_MIX, 0, d, nbc, nb))
        in_specs.append(pl.BlockSpec((None, BLOCK, z.shape[2]),
                                     lambda i, j, d=d: (i, _block_of(j, d, nbc, nb), 0)))
    const = lambda a: pl.BlockSpec(a.shape, lambda i, j: (0,) * a.ndim)
    in_specs += [const(conv), const(vecs), const(wup), const(aup), const(gup)]
    out_d = lambda d: pl.BlockSpec((None, BLOCK, C_MIX), lambda i, j: (i, _block_of(j, d, nbc, nb), 0))
    shp = jax.ShapeDtypeStruct((b, l, C_MIX), F32)
    return pl.pallas_call(
        functools.partial(_rwkv_kernel, nbc, nb),
        grid=(b, nb),
        in_specs=in_specs,
        out_specs=[out_d(0), out_d(1), out_d(0), out_d(0)],
        out_shape=[shp, shp, shp, shp],
        scratch_shapes=[pltpu.VMEM((2, CHUNK, C_MIX), F32)],
        compiler_params=pltpu.CompilerParams(
            dimension_semantics=("arbitrary", "arbitrary"), vmem_limit_bytes=VMEM_LIMIT),
        name="rwkv7_chunked",
    )(rkv, rkv, rkv, z, rkv, rkv, rkv, z, conv, vecs, wup, aup, gup)


def _row_cumsum_matrix(d, bd):
    src = _iota((C_MIX, C_MIX), 0) & (CHUNK - 1)
    dst = _iota((C_MIX, C_MIX), 1) & (CHUNK - 1)
    keep = (src <= dst) if d == 0 else (src >= dst)
    return jnp.where(bd & keep, 1.0, 0.0)


def _log_sigmoid(x):
    return jnp.minimum(x, 0.0) - jnp.log1p(jnp.exp(-jnp.abs(x)))


def _mlstm_terms(bd, bd_ones, items):
    lane_h = _iota((CHUNK, C_MIX), 1) >> 6
    d_, q, k, v, b_exp, i_exp, b_row, i_row = [list(x) for x in zip(*items)]
    incl = [_tri_masks(d)[0] for d in d_]
    bend = [b[CHUNK - 1:CHUNK, :] if d == 0 else b[0:1, :] for d, b in zip(d_, b_exp)]
    logd = _each(lambda m, b, br, ir: jnp.where(m, b - br + ir, NEG_INF), incl, b_exp, b_row, i_row)

    def seg_max(x):
        out = jnp.zeros((CHUNK, C_MIX), F32)
        for h in range(N_HEADS):
            mx = jnp.max(jnp.where(lane_h == h, x, NEG_INF), axis=-1, keepdims=True)
            out = jnp.where(lane_h == h, mx, out)
        return out

    mloc = _each(seg_max, logd)
    qk = _each(lambda x, y: _dot_nt(x, _bdstack(y, bd)), q, k)
    sw = _each(lambda s, l, m: s * jnp.exp(l - m), qk, logd, mloc)
    num_loc = _each(lambda s, x: _dot(s, _bdstack(x, bd)), sw, v)
    den_loc = _each(lambda s: _dot(s, bd_ones), sw)
    g_col = _each(lambda e, b, i: e - b + i, bend, b_exp, i_exp)
    mg = _each(lambda g: jnp.max(g, axis=0, keepdims=True), g_col)
    kw = _each(lambda x, g, m: x * jnp.exp(g - m), k, g_col, mg)
    kg_cat = _each(lambda x, y: _fold(_dot_tn(x, y)), kw, v)
    kn = _each(lambda x: jnp.sum(x, axis=0, keepdims=True), kw)
    return list(zip(bend, mloc, num_loc, den_loc, mg, kg_cat, kn))


def _mlstm_kernel(nbc, nb, *refs):
    (x0, p0, n0, v0, g0, gr0, x1, p1, n1, v1, g1, gr1, conv_ref, bcol_ref, brow_ref,
     h0_ref, h1_ref, c_ref, n_ref, m_ref) = refs
    j = pl.program_id(1)

    @pl.when(j == 0)
    def _():
        c_ref[...] = jnp.zeros_like(c_ref)
        n_ref[...] = jnp.zeros_like(n_ref)
        m_ref[...] = jnp.zeros_like(m_ref)

    bd = _bd_mask()
    bd_ones = jnp.where(bd, 1.0, 0.0)
    ins = ((x0, p0, n0, v0, g0, gr0), (x1, p1, n1, v1, g1, gr1))
    h_refs = (h0_ref, h1_ref)
    items = []
    for d in range(2):
        x_ref, p_ref, nx_ref, v_ref, g_ref, gr_ref = ins[d]
        first, last = _stream_ends(_block_of(j, d, nbc, nb), nbc, nb)
        cv = _conv3(x_ref, p_ref, nx_ref, conv_ref, first, last)
        qk = cv * jax.nn.sigmoid(cv)
        q = qk[:, 0:C_MIX]
        k = qk[:, C_MIX:] * (HEAD_DIM ** -0.5)
        v = v_ref[...]
        gcol = g_ref[...] + bcol_ref[...]
        i_exp = _expand_cols(gcol, 4 * d)
        f_exp = _log_sigmoid(_expand_cols(gcol, 8 + 4 * d))
        b_exp = _dot_l01(_block_cumsum_matrix(d), f_exp)
        i_rows = gr_ref[d] + brow_ref[d]
        f_rows = _log_sigmoid(gr_ref[2 + d] + brow_ref[2 + d])
        f_pad = jnp.concatenate([f_rows] * (HALO // GROUP), axis=0)
        b_rows = _dot_r01_full(f_pad, _row_cumsum_matrix(d, bd))
        for g in range(GROUP):
            sl = slice(g * CHUNK, (g + 1) * CHUNK)
            items.append((d, q[sl], k[sl], v[sl], b_exp[sl], i_exp[sl], b_rows[g:g + 1, :], i_rows[g:g + 1, :]))
    terms = _mlstm_terms(bd, bd_ones, items)
    m_prev = [m_ref[0][0:1, :], m_ref[1][0:1, :]]
    n_prev = [n_ref[0][0:1, :], n_ref[1][0:1, :]]
    c_prev = [c_ref[0], c_ref[1]]
    for step in range(GROUP):
        chunk = (step, GROUP - 1 - step)
        for d in range(2):
            g = chunk[d]
            q, b_exp = items[d * GROUP + g][1], items[d * GROUP + g][4]
            bend, mloc, num_loc, den_loc, mg, kg_cat, kn = terms[d * GROUP + g]
            inter = b_exp + m_prev[d]
            m_t = jnp.maximum(inter, mloc)
            a_loc = jnp.exp(mloc - m_t)
            w_int = jnp.exp(inter - m_t)
            num = a_loc * num_loc + w_int * _dot(q, _bdstack(c_prev[d], bd))
            den = a_loc * den_loc + w_int * _dot(q * n_prev[d], bd_ones)
            h_refs[d][g * CHUNK:(g + 1) * CHUNK, :] = num / jnp.maximum(jnp.abs(den), jnp.exp(-m_t))
            m_new = jnp.maximum(bend + m_prev[d], mg)
            sc = jnp.exp(mg - m_new)
            dec = jnp.exp(bend + m_prev[d] - m_new)
            c_prev[d] = dec * c_prev[d] + sc * kg_cat
            n_prev[d] = dec * n_prev[d] + sc * kn
            m_prev[d] = m_new
    for d in range(2):
        c_ref[d] = c_prev[d]
        n_ref[d] = jnp.broadcast_to(n_prev[d], (HALO, C_MIX))
        m_ref[d] = jnp.broadcast_to(m_prev[d], (HALO, C_MIX))


def _mlstm_call(qk, vo, gt, grow, conv, bcol, brow, t_ctx):
    b, l, _ = qk.shape
    nb, nbc = l // BLOCK, t_ctx // BLOCK
    in_specs = []
    for d in range(2):
        bidx = lambda i, j, d=d: (i, _block_of(j, d, nbc, nb), 0)
        in_specs += list(_block_specs(2 * C_MIX, 0, d, nbc, nb))
        in_specs.append(pl.BlockSpec((None, BLOCK, C_MIX), bidx))
        in_specs.append(pl.BlockSpec((None, BLOCK, gt.shape[2]), bidx))
        in_specs.append(pl.BlockSpec((None, None, 4, GROUP, C_MIX),
                                     lambda i, j, d=d: (i, _block_of(j, d, nbc, nb), 0, 0, 0)))
    const = lambda a: pl.BlockSpec(a.shape, lambda i, j: (0,) * a.ndim)
    in_specs += [const(conv), const(bcol), const(brow)]
    out_d = lambda d: pl.BlockSpec((None, BLOCK, C_MIX), lambda i, j: (i, _block_of(j, d, nbc, nb), 0))
    shp = jax.ShapeDtypeStruct((b, l, C_MIX), F32)
    return pl.pallas_call(
        functools.partial(_mlstm_kernel, nbc, nb),
        grid=(b, nb),
        in_specs=in_specs,
        out_specs=[out_d(0), out_d(1)],
        out_shape=[shp, shp],
        scratch_shapes=[pltpu.VMEM((2, CHUNK, C_MIX), F32),
                        pltpu.VMEM((2, HALO, C_MIX), F32),
                        pltpu.VMEM((2, HALO, C_MIX), F32)],
        compiler_params=pltpu.CompilerParams(
            dimension_semantics=("arbitrary", "arbitrary"), vmem_limit_bytes=VMEM_LIMIT),
        name="mlstm_chunked",
    )(qk, qk, qk, vo, gt, grow, qk, qk, qk, vo, gt, grow, conv, bcol, brow)


def _attn_kernel(t_ctx, qb0, n_lat, sink_ref, q_ref, kc_ref, vc_ref, ka_ref, kb_ref, kd_ref,
                 va_ref, vb_ref, vd_ref, o_ref):
    qb = pl.program_id(1) + qb0
    ncb = t_ctx // ATTN_BLOCK
    lb = qb - ncb
    bq = ATTN_BLOCK
    n_heads = ATTN_KV_HEADS * ATTN_GROUP
    kmat = jnp.concatenate([kc_ref[...], ka_ref[...], kb_ref[...], kd_ref[...]], axis=0)
    vmat = jnp.concatenate([vc_ref[...], va_ref[...], vb_ref[...], vd_ref[...]], axis=0)
    nk = kmat.shape[0]

    lane_lo = _iota((bq, 2 * HEAD_DIM), 1) < HEAD_DIM
    scale = HEAD_DIM ** -0.5
    blocks = []
    for hq in range(n_heads):
        kvh = hq // ATTN_GROUP
        tile = q_ref[:, (hq // 2) * 128:(hq // 2 + 1) * 128] * scale
        if hq % 2 != kvh:
            tile = pltpu.roll(tile, HEAD_DIM, 1)
        blocks.append(jnp.where(lane_lo, tile, 0.0) if kvh == 0 else jnp.where(lane_lo, 0.0, tile))
    qs = jnp.concatenate(blocks, axis=0)

    q_pos = _iota((bq, nk), 0) + bq
    col = _iota((bq, nk), 1)
    k_rel = col - t_ctx
    k_abs = k_rel + (lb - 1) * bq
    valid_loc = (jnp.abs(k_rel - q_pos) <= WINDOW) & (k_abs >= 0) & (k_abs < n_lat) & (lb >= 0)
    valid = (col < t_ctx) | valid_loc

    s = _dot_nt(qs, kmat).reshape(n_heads, bq, nk)
    s = jnp.where(valid[None], s, NEG_INF).reshape(n_heads * bq, nk)
    sk = sink_ref[...]
    m = jnp.maximum(jnp.max(s, axis=-1, keepdims=True), sk)
    e = jnp.exp(s - m)
    ov = _dot(e, jnp.concatenate([vmat, jnp.ones_like(vmat)], axis=1))
    o = ov[:, 0:128] / (ov[:, 128:256] + jnp.exp(sk - m))
    for pair in range(n_heads // 2):
        kvh = (2 * pair) // ATTN_GROUP
        even = o[(2 * pair) * bq:(2 * pair + 1) * bq]
        odd = o[(2 * pair + 1) * bq:(2 * pair + 2) * bq]
        if kvh == 0:
            tile = jnp.where(lane_lo, even, pltpu.roll(odd, HEAD_DIM, 1))
        else:
            tile = jnp.where(lane_lo, pltpu.roll(even, HEAD_DIM, 1), odd)
        o_ref[:, pair * 128:(pair + 1) * 128] = tile


def _attn_call(qkv, sink, t_ctx, with_ctx_queries):
    b, l, _ = qkv.shape
    nqb = l // ATTN_BLOCK
    ncb = t_ctx // ATTN_BLOCK
    qb0 = 0 if with_ctx_queries else ncb
    kcol, vcol = C_A // 128, C_A // 128 + 1
    sink_rows = jnp.repeat(sink.astype(F32), ATTN_BLOCK).reshape(-1, 1)

    def loc(off, col):
        return pl.BlockSpec((None, ATTN_BLOCK, 128),
                            lambda i, t: (i, jnp.clip(t + qb0 + off, ncb, nqb - 1), col))

    return pl.pallas_call(
        functools.partial(_attn_kernel, t_ctx, qb0, l - t_ctx),
        grid=(b, nqb - qb0),
        in_specs=[pl.BlockSpec(sink_rows.shape, lambda i, t: (0, 0)),
                  pl.BlockSpec((None, ATTN_BLOCK, C_A), lambda i, t: (i, t + qb0, 0)),
                  pl.BlockSpec((None, t_ctx, 128), lambda i, t: (i, 0, kcol)),
                  pl.BlockSpec((None, t_ctx, 128), lambda i, t: (i, 0, vcol)),
                  loc(-1, kcol), loc(0, kcol), loc(1, kcol),
                  loc(-1, vcol), loc(0, vcol), loc(1, vcol)],
        out_specs=pl.BlockSpec((None, ATTN_BLOCK, C_A), lambda i, t: (i, t + qb0, 0)),
        out_shape=jax.ShapeDtypeStruct((b, l, C_A), F32),
        compiler_params=pltpu.CompilerParams(
            dimension_semantics=("arbitrary", "arbitrary"), vmem_limit_bytes=VMEM_LIMIT),
        name="attn_window",
    )(sink_rows, qkv, qkv, qkv, qkv, qkv, qkv, qkv, qkv, qkv)


def _outmlp_kernel(t_ctx, row0, final, x_ref, y0_ref, y1_ref, bonus_ref, gate_ref, ya_ref,
                   h0_ref, h1_ref, og_ref, mc_ref, ml_ref, g2_ref, lnv_ref, fin_ref,
                   wo_ref, w1_ref, w2_ref, o_ref):
    tm = x_ref.shape[0]
    row = row0 + pl.program_id(1) * tm + _iota((tm, 1), 0)
    is_ctx = row < t_ctx
    mod = lambda i: jnp.where(is_ctx, mc_ref[i:i + 1, :], ml_ref[i:i + 1, :])
    bdm = jnp.where(_bd_mask(), 1.0 / HEAD_DIM, 0.0)

    yr = _head_stats_norm(y0_ref[...] + y1_ref[...], bdm, RWKV_LN_EPS) * lnv_ref[0:1, :] + lnv_ref[1:2, :]
    yr = (yr + bonus_ref[...]) * gate_ref[...]
    ym = _head_stats_norm(h0_ref[...] + h1_ref[...], bdm, MLSTM_LN_EPS) * lnv_ref[2:3, :]
    ym = jax.nn.sigmoid(og_ref[...]) * ym
    mix = (_dot(yr.astype(BF16), wo_ref[0:C_MIX, :])
           + _dot(ya_ref[...].astype(BF16), wo_ref[C_MIX:C_MIX + C_A, :])
           + _dot(ym.astype(BF16), wo_ref[C_MIX + C_A:, :]))
    x1 = x_ref[...] + mod(0) * mix
    h2 = (_rms(x1, g2_ref[...]) * (1.0 + mod(2)) + mod(1)).astype(BF16)
    d_ff = w1_ref.shape[1]
    step = 1024
    acc = jnp.zeros_like(x1)
    for c0 in range(0, d_ff, step):
        hid = jnp.maximum(_dot(h2, w1_ref[:, c0:c0 + step]), 0.0)
        acc = acc + _dot((hid * hid).astype(BF16), w2_ref[c0:c0 + step, :])
    x2 = x1 + mod(3) * acc
    if final:
        x2 = _rms(x2, fin_ref[...])
    o_ref[...] = x2


def _outmlp_call(x, y0, y1, bonus, gate, ya, h0, h1, vo, modc, modl, g2, lnv, fin, wo, w1, w2,
                 t_ctx, tm, final):
    b, l, d = x.shape
    t0 = t_ctx // tm if final else 0
    n_t = l // tm - t0
    row = lambda n, col=0: pl.BlockSpec((None, tm, n), lambda i, t: (i, t + t0, col))
    const = lambda a: pl.BlockSpec(a.shape, lambda i, t: (0,) * a.ndim, pipeline_mode=pl.Buffered(1))
    small = lambda a: pl.BlockSpec(a.shape, lambda i, t: (0,) * a.ndim)
    out_rows = l - t0 * tm
    return pl.pallas_call(
        functools.partial(_outmlp_kernel, t_ctx, t0 * tm, final),
        grid=(b, n_t),
        in_specs=[row(d), row(C_MIX), row(C_MIX), row(C_MIX), row(C_MIX), row(C_A),
                  row(C_MIX), row(C_MIX), row(C_MIX, 1),
                  small(modc), pl.BlockSpec((None, 4, d), lambda i, t: (i, 0, 0)),
                  small(g2), small(lnv), small(fin), const(wo), const(w1), const(w2)],
        out_specs=pl.BlockSpec((None, tm, d), lambda i, t: (i, t, 0)),
        out_shape=jax.ShapeDtypeStruct((b, out_rows, d), F32),
        compiler_params=pltpu.CompilerParams(
            dimension_semantics=("arbitrary", "arbitrary"), vmem_limit_bytes=VMEM_LIMIT),
        name="outproj_mlp",
    )(x, y0, y1, bonus, gate, ya, h0, h1, vo, modc, modl, g2, lnv, fin, wo, w1, w2)


def _rope_tables(s, t_ctx):
    rows = s // GRID_W
    n_freq = HEAD_DIM // 4
    row = jnp.repeat(jnp.arange(rows), GRID_W).astype(F32)
    col = jnp.tile(jnp.arange(GRID_W), rows).astype(F32)
    inv = ROPE_BASE ** (-jnp.arange(n_freq, dtype=F32) / n_freq)
    ang = jnp.concatenate([row[:, None] * inv, col[:, None] * inv], axis=-1)
    cos, sin = jnp.cos(ang), jnp.sin(ang)
    cos_l = jnp.concatenate([cos, cos, cos, cos], axis=-1)
    sin_l = jnp.concatenate([-sin, sin, -sin, sin], axis=-1)
    cos_t = jnp.concatenate([jnp.ones((t_ctx, 128), F32), cos_l], axis=0)
    sin_t = jnp.concatenate([jnp.zeros((t_ctx, 128), F32), sin_l], axis=0)
    return cos_t, sin_t


def _pad_dir_lora(w):
    z = jnp.zeros_like(w[0])
    return jnp.stack([jnp.concatenate([w[0], z], axis=0), jnp.concatenate([z, w[1]], axis=0)], axis=0)


def kernel(x, c, ctx, c_ctx, ada_w, ada_b, norm1_g, norm2_g, w_in, w_out, rwkv_conv, rwkv_w0, rwkv_w_up, rwkv_a0, rwkv_a_up, rwkv_g_up, rwkv_k_k, rwkv_k_a, rwkv_r_k, rwkv_ln_w, rwkv_ln_b, attn_sink, mlstm_conv, mlstm_b_i, mlstm_b_f, mlstm_norm_g, mlp_w1, mlp_w2, final_g):
    b, s, d = x.shape
    t_ctx = ctx.shape[1]
    depth = ada_w.shape[0]
    l = t_ctx + s
    assert s % BLOCK == 0 and t_ctx % BLOCK == 0 and s % GRID_W == 0
    tm = 256
    assert l % tm == 0 and t_ctx % tm == 0

    n_mod_rows = -(-(b + 1) // HALO) * HALO
    cc = jnp.zeros((n_mod_rows, d), F32).at[:b].set(c).at[b].set(c_ctx)
    mods = _mods_call(cc, ada_w, ada_b).reshape(depth, n_mod_rows, 6, d)
    cos_t, sin_t = _rope_tables(s, t_ctx)
    xc = jnp.concatenate([ctx, x], axis=1)

    splits = [0, 3 * C_MIX]
    for w_ in (4 * R_LORA + R_GATE, C_A + 2 * C_KV, 2 * C_MIX, 2 * C_MIX, 4 * N_HEADS):
        splits.append(splits[-1] + w_)

    for li in range(depth):
        last = li == depth - 1
        m_l = mods[li, :b]
        m_c = mods[li, b]
        wl = w_in[li].astype(BF16)
        ws = [wl[:, splits[i]:splits[i + 1]] for i in range(6)]
        ws.append(jnp.transpose(ws[5]))
        rkv, z, qkv, qk, vo, gt, gtt = _inproj_call(
            xc, m_c[0:2], m_l[:, 0:2], norm1_g[li].reshape(1, d), cos_t, sin_t, ws, t_ctx, tm)

        vecs = jnp.stack([rwkv_k_k[li], rwkv_k_a[li], rwkv_r_k[li], rwkv_w0[li, 0], rwkv_w0[li, 1],
                          rwkv_a0[li, 0], rwkv_a0[li, 1], jnp.zeros((C_MIX,), F32)], axis=0)
        y0, y1, bonus, gate = _rwkv_call(rkv, z, rwkv_conv[li], vecs, _pad_dir_lora(rwkv_w_up[li]),
                                         _pad_dir_lora(rwkv_a_up[li]), rwkv_g_up[li], t_ctx)

        ya = _attn_call(qkv, attn_sink[li], t_ctx, not last)

        grow = (gtt.reshape(b, 4, N_HEADS, l // BLOCK, GROUP, CHUNK).transpose(0, 3, 1, 4, 2, 5)
                .reshape(b, l // BLOCK, 4, GROUP, C_MIX))
        bias = jnp.concatenate([mlstm_b_i[li].reshape(-1), mlstm_b_f[li].reshape(-1)])
        bcol = bias.reshape(1, 4 * N_HEADS)
        brow = jnp.repeat(bias.reshape(4, 1, N_HEADS), CHUNK, axis=2)
        h0, h1 = _mlstm_call(qk, vo, gt, grow, mlstm_conv[li], bcol, brow, t_ctx)

        lnv = jnp.stack([rwkv_ln_w[li], rwkv_ln_b[li], mlstm_norm_g[li], jnp.zeros((C_MIX,), F32)], axis=0)
        lnv = jnp.concatenate([lnv, jnp.zeros((4, C_MIX), F32)], axis=0)
        xc = _outmlp_call(xc, y0, y1, bonus, gate, ya, h0, h1, vo, m_c[2:6], m_l[:, 2:6],
                          norm2_g[li].reshape(1, d), lnv, final_g.reshape(1, d),
                          w_out[li].astype(BF16), mlp_w1[li].astype(BF16), mlp_w2[li].astype(BF16),
                          t_ctx, tm, last)
    return xc
```

```python
import functools

import jax
import jax.numpy as jnp
from jax import lax
from jax.experimental import pallas as pl
from jax.experimental.pallas import tpu as pltpu

F32 = jnp.float32
BF16 = jnp.bfloat16

HEAD_DIM = 64
N_HEADS = 4
C_MIX = N_HEADS * HEAD_DIM
ATTN_KV_HEADS = 2
ATTN_GROUP = 4
C_A = ATTN_KV_HEADS * ATTN_GROUP * HEAD_DIM
C_KV = ATTN_KV_HEADS * HEAD_DIM
GRID_W = 64
ATTN_BLOCK = 128
WINDOW = 128
ROPE_BASE = 10000.0
R_LORA = 64
R_GATE = 128
DECAY_SCALE = 0.606531
RMS_EPS = 1e-6
RWKV_LN_EPS = 64e-5
MLSTM_LN_EPS = 1e-6
NEG_INF = -1e30
CHUNK = 64
GROUP = 4
BLOCK = GROUP * CHUNK
HALO = 8
VMEM_LIMIT = 56 * 1024 * 1024


def _dot(a, b):
    return jnp.dot(a, b, preferred_element_type=F32)


def _dot_nt(a, b):
    return lax.dot_general(a, b, (((1,), (1,)), ((), ())), preferred_element_type=F32)


def _dot_tn(a, b):
    return lax.dot_general(a, b, (((0,), (0,)), ((), ())), preferred_element_type=F32)


def _split2(a):
    hi = a.astype(BF16)
    lo = (a - hi.astype(F32)).astype(BF16)
    return hi, lo


def _split3(a):
    a1 = a.astype(BF16)
    r1 = a - a1.astype(F32)
    a2 = r1.astype(BF16)
    a3 = (r1 - a2.astype(F32)).astype(BF16)
    return a1, a2, a3


def _dot_r01(a, m01):
    hi, lo = _split2(a)
    mb = m01.astype(BF16)
    return _dot(hi, mb) + _dot(lo, mb)


def _dot_l01(m01, a):
    a1, a2, a3 = _split3(a)
    mb = m01.astype(BF16)
    return _dot(mb, a1) + _dot(mb, a2) + _dot(mb, a3)


def _dot_r01_full(a, m01):
    a1, a2, a3 = _split3(a)
    mb = m01.astype(BF16)
    return _dot(a1, mb) + _dot(a2, mb) + _dot(a3, mb)


def _dot3(a, b):
    ah, al = _split2(a)
    bh, bl = _split2(b)
    return _dot(ah, bh) + _dot(al, bh) + _dot(ah, bl)


def _iota(shape, axis):
    return lax.broadcasted_iota(jnp.int32, shape, axis)


def _bd_mask():
    return (_iota((C_MIX, C_MIX), 0) >> 6) == (_iota((C_MIX, C_MIX), 1) >> 6)


def _bdstack(x, bd):
    return jnp.where(bd, jnp.concatenate([x, x, x, x], axis=0), 0.0)


def _fold(f):
    lane_h = _iota((CHUNK, C_MIX), 1) >> 6
    out = jnp.where(lane_h == 0, f[0:CHUNK], 0.0)
    for h in range(1, N_HEADS):
        out = jnp.where(lane_h == h, f[h * CHUNK:(h + 1) * CHUNK], out)
    return out


def _expand_cols(g, first):
    lane_h = _iota((g.shape[0], C_MIX), 1) >> 6
    out = jnp.zeros((g.shape[0], C_MIX), F32)
    for h in range(N_HEADS):
        out = jnp.where(lane_h == h, g[:, first + h:first + h + 1], out)
    return out


def _conv3(x_ref, prev_ref, next_ref, w_ref, first, last):
    xt = x_ref[...]
    n = xt.shape[0]
    rows = _iota((n, 1), 0)
    prev = prev_ref[HALO - 1:HALO, :] * jnp.where(first, 0.0, 1.0)
    nxt = next_ref[0:1, :] * jnp.where(last, 0.0, 1.0)
    up = jnp.where(rows == 0, prev, pltpu.roll(xt, 1, 0))
    dn = jnp.where(rows == n - 1, nxt, pltpu.roll(xt, n - 1, 0))
    return up * w_ref[0:1, :] + xt * w_ref[1:2, :] + dn * w_ref[2:3, :]


def _rms(x, g):
    return x * lax.rsqrt(jnp.mean(x * x, axis=-1, keepdims=True) + RMS_EPS) * g


def _head_stats_norm(y, bdm, eps):
    mu = _dot_r01(y, bdm)
    yc = y - mu
    var = _dot_r01(yc * yc, bdm)
    return yc * lax.rsqrt(var + eps)


def _mods_kernel(c_ref, w_ref, b_ref, o_ref):
    c = c_ref[...]
    s = c * jax.nn.sigmoid(c)
    o_ref[...] = _dot(s, w_ref[...]) + b_ref[...]


def _mods_call(cc, ada_w, ada_b):
    depth, d, n = ada_w.shape
    rows = cc.shape[0]
    tn = 1536
    return pl.pallas_call(
        _mods_kernel,
        grid=(depth, n // tn),
        in_specs=[pl.BlockSpec((rows, d), lambda l, j: (0, 0)),
                  pl.BlockSpec((None, d, tn), lambda l, j: (l, 0, j)),
                  pl.BlockSpec((None, 1, tn), lambda l, j: (l, 0, j))],
        out_specs=pl.BlockSpec((None, rows, tn), lambda l, j: (l, 0, j)),
        out_shape=jax.ShapeDtypeStruct((depth, rows, n), F32),
        compiler_params=pltpu.CompilerParams(vmem_limit_bytes=VMEM_LIMIT),
        name="ada_mods",
    )(cc, ada_w, ada_b.reshape(depth, 1, n))


def _inproj_kernel(t_ctx, x_ref, mc_ref, ml_ref, g_ref, cos_ref, sin_ref,
                   w_rkv, w_z, w_qkv, w_qk, w_vo, w_gt, w_gtt,
                   o_rkv, o_z, o_qkv, o_qk, o_vo, o_gt, o_gtt):
    tm = x_ref.shape[0]
    row = pl.program_id(1) * tm + _iota((tm, 1), 0)
    is_ctx = row < t_ctx
    shift = jnp.where(is_ctx, mc_ref[0:1, :], ml_ref[0:1, :])
    scale = jnp.where(is_ctx, mc_ref[1:2, :], ml_ref[1:2, :])
    h = _rms(x_ref[...], g_ref[...]) * (1.0 + scale) + shift
    hb = h.astype(BF16)
    o_rkv[...] = _dot(hb, w_rkv[...])
    o_z[...] = _dot(hb, w_z[...])
    o_qk[...] = _dot(hb, w_qk[...])
    o_vo[...] = _dot(hb, w_vo[...])
    o_gt[...] = _dot(hb, w_gt[...])
    o_gtt[...] = _dot_nt(w_gtt[...], hb)
    qkv = _dot(hb, w_qkv[...])
    cos = cos_ref[...]
    sin = sin_ref[...]
    first_half = (_iota((tm, 128), 1) & 63) < 32
    n_rot = (C_A + C_KV) // 128
    for i in range(n_rot):
        blk = qkv[:, i * 128:(i + 1) * 128]
        partner = jnp.where(first_half, pltpu.roll(blk, 96, 1), pltpu.roll(blk, 32, 1))
        o_qkv[:, i * 128:(i + 1) * 128] = blk * cos + partner * sin
    o_qkv[:, n_rot * 128:] = qkv[:, n_rot * 128:]


def _inproj_call(x, modc, modl, g, cos_t, sin_t, ws, t_ctx, tm):
    b, l, d = x.shape
    widths = [w.shape[1] for w in ws[:6]]
    row_spec = lambda n: pl.BlockSpec((None, tm, n), lambda i, t: (i, t, 0))
    const = lambda a: pl.BlockSpec(a.shape, lambda i, t: (0,) * a.ndim)
    out_shape = [jax.ShapeDtypeStruct((b, l, n), F32) for n in widths]
    out_shape.append(jax.ShapeDtypeStruct((b, widths[5], l), F32))
    out_specs = [row_spec(n) for n in widths]
    out_specs.append(pl.BlockSpec((None, widths[5], tm), lambda i, t: (i, 0, t)))
    return pl.pallas_call(
        functools.partial(_inproj_kernel, t_ctx),
        grid=(b, l // tm),
        in_specs=[row_spec(d), const(modc),
                  pl.BlockSpec((None, 2, d), lambda i, t: (i, 0, 0)),
                  const(g),
                  pl.BlockSpec((tm, 128), lambda i, t: (t, 0)),
                  pl.BlockSpec((tm, 128), lambda i, t: (t, 0))] + [const(w) for w in ws],
        out_specs=out_specs,
        out_shape=out_shape,
        compiler_params=pltpu.CompilerParams(
            dimension_semantics=("arbitrary", "arbitrary"), vmem_limit_bytes=VMEM_LIMIT),
        name="inproj",
    )(x, modc, modl, g, cos_t, sin_t, *ws)


def _block_of(j, d, nbc, nb):
    if d == 0:
        return j
    return jnp.where(j < nbc, nbc - 1 - j, nb + nbc - 1 - j)


def _block_specs(width, col, d, nbc, nb):
    per = BLOCK // HALO
    last_halo = nb * per - 1
    blk = pl.BlockSpec((None, BLOCK, width), lambda b, j: (b, _block_of(j, d, nbc, nb), col))
    prev = pl.BlockSpec((None, HALO, width),
                        lambda b, j: (b, jnp.maximum(_block_of(j, d, nbc, nb) * per - 1, 0), col))
    nxt = pl.BlockSpec((None, HALO, width),
                       lambda b, j: (b, jnp.minimum(_block_of(j, d, nbc, nb) * per + per, last_halo), col))
    return blk, prev, nxt


def _stream_ends(blk, nbc, nb):
    first = jnp.logical_or(blk == 0, blk == nbc)
    last = jnp.logical_or(blk == nbc - 1, blk == nb - 1)
    return first, last


def _tri_masks(d):
    t = _iota((CHUNK, C_MIX), 0)
    s = _iota((CHUNK, C_MIX), 1) & (CHUNK - 1)
    if d == 0:
        incl, strict = s <= t, s < t
    else:
        incl, strict = s >= t, s > t
    return incl, strict, s == t, (s >> 4) == (t >> 4)


def _block_cumsum_matrix(d):
    t = _iota((BLOCK, BLOCK), 0)
    s = _iota((BLOCK, BLOCK), 1)
    same = (t >> 6) == (s >> 6)
    keep = (s <= t) if d == 0 else (s >= t)
    return jnp.where(same & keep, 1.0, 0.0)


def _each(f, *lists):
    return [f(*xs) for xs in zip(*lists)]


def _rwkv_terms(bd, items):
    hmm = lambda ps, qs: [_dot(p, _bdstack(q, bd)) for p, q in zip(ps, qs)]
    plus = lambda xs, ys: _each(lambda x, y: x + y, xs, ys)
    minus = lambda xs, ys: _each(lambda x, y: x - y, xs, ys)
    times = lambda xs, ys: _each(lambda x, y: x * y, xs, ys)
    d_, r, kk, kd, bb, v, c, lw = [list(x) for x in zip(*items)]
    masks = [_tri_masks(d) for d in d_]
    incl = [m[0] for m in masks]
    strict = [m[1] for m in masks]
    eye_f = [jnp.where(m[2], 1.0, 0.0) for m in masks]
    blk16 = [m[3] for m in masks]
    ctot = [ci[CHUNK - 1:CHUNK, :] if d == 0 else ci[0:1, :] for d, ci in zip(d_, c)]
    rt = _each(lambda x, ci: x * jnp.exp(ci), r, c)
    at = _each(lambda x, ci, li: x * jnp.exp(ci - li), kk, c, lw)
    e_neg = _each(lambda ci: jnp.exp(-ci), c)
    kh = times(kd, e_neg)
    bh = times(bb, e_neg)
    e_end = _each(lambda ti, ci: jnp.exp(ti - ci), ctot, c)
    kbar = times(kd, e_end)
    bbar = times(bb, e_end)
    lhs = _each(lambda x, y: jnp.concatenate([x, y], axis=0), rt, at)
    a_k = _each(lambda x, y: _dot_nt(x, _bdstack(y, bd)), lhs, kh)
    a_b = _each(lambda x, y: _dot_nt(x, _bdstack(y, bd)), lhs, bh)
    a_rk = _each(lambda m, x: jnp.where(m, x[0:CHUNK], 0.0), incl, a_k)
    a_rb = _each(lambda m, x: jnp.where(m, x[0:CHUNK], 0.0), incl, a_b)
    a_ak = _each(lambda m, x: jnp.where(m, x[CHUNK:], 0.0), strict, a_k)
    nmat = _each(lambda m, x: jnp.where(m, x[CHUNK:], 0.0), strict, a_b)
    nd = _each(lambda m, x: jnp.where(m, x, 0.0), blk16, nmat)
    no = minus(nmat, nd)
    p2 = hmm(nd, nd)
    xd = hmm(minus(eye_f, nd), plus(eye_f, p2))
    p4 = hmm(p2, p2)
    xd = hmm(xd, plus(eye_f, p4))
    p8 = hmm(p4, p4)
    xd = hmm(xd, plus(eye_f, p8))
    pm = hmm(xd, no)
    pp = hmm(pm, pm)
    tinv = hmm(hmm(minus(eye_f, pm), plus(eye_f, pp)), xd)
    v_st = [_bdstack(x, bd) for x in v]
    w1 = hmm(tinv, at)
    ta = hmm(tinv, a_ak)
    u = _each(_dot, ta, v_st)
    qp = minus(rt, hmm(a_rb, w1))
    y_loc = minus(_each(_dot, a_rk, v_st), hmm(a_rb, u))
    m_cat = _each(lambda e, t, x, y: e * jnp.exp(t) - _fold(_dot_tn(x, y)), eye_f, ctot, bbar, w1)
    g_cat = _each(lambda x, y, z, w: _fold(_dot_tn(x, y) - _dot_tn(z, w)), kbar, v, bbar, u)
    return list(zip(qp, y_loc, m_cat, g_cat))


def _rwkv_kernel(nbc, nb, *refs):
    (x0, p0, n0, z0, x1, p1, n1, z1, conv_ref, vec_ref, wup_ref, aup_ref, gup_ref,
     y0_ref, y1_ref, bonus_ref, gate_ref, s_ref) = refs
    j = pl.program_id(1)

    @pl.when(j == 0)
    def _():
        s_ref[...] = jnp.zeros_like(s_ref)

    bd = _bd_mask()
    bd_ones = jnp.where(bd, 1.0, 0.0)
    k_k = vec_ref[0:1, :]
    k_a = vec_ref[1:2, :]
    r_k = vec_ref[2:3, :]
    ins = ((x0, p0, n0, z0), (x1, p1, n1, z1))
    y_refs = (y0_ref, y1_ref)
    items = []
    for d in range(2):
        x_ref, p_ref, n_ref, z_ref = ins[d]
        first, last = _stream_ends(_block_of(j, d, nbc, nb), nbc, nb)
        cv = _conv3(x_ref, p_ref, n_ref, conv_ref, first, last)
        r = cv[:, 0:C_MIX]
        k = cv[:, C_MIX:2 * C_MIX]
        v = cv[:, 2 * C_MIX:3 * C_MIX]
        kk = k * k_k
        kk = kk * lax.rsqrt(_dot_r01(kk * kk, bd_ones) + 1e-12)
        zt = z_ref[...]
        zw = zt[:, 0:2 * R_LORA]
        za = zt[:, 2 * R_LORA:4 * R_LORA]
        dw = vec_ref[3 + d:4 + d, :] + _dot(jnp.tanh(zw), wup_ref[d])
        lw = -DECAY_SCALE * jax.nn.sigmoid(dw)
        a = jax.nn.sigmoid(vec_ref[5 + d:6 + d, :] + _dot(za, aup_ref[d]))
        kd = k * (1.0 + (a - 1.0) * k_a)
        bb = kk * a
        if d == 0:
            zg = zt[:, 4 * R_LORA:4 * R_LORA + R_GATE]
            gate_ref[...] = _dot(jax.nn.sigmoid(zg), gup_ref[...])
            bonus_ref[...] = _dot_r01(r * k * r_k, bd_ones) * v
        c = _dot_l01(_block_cumsum_matrix(d), lw)
        for g in range(GROUP):
            sl = slice(g * CHUNK, (g + 1) * CHUNK)
            items.append((d, r[sl], kk[sl], kd[sl], bb[sl], v[sl], c[sl], lw[sl]))
    terms = _rwkv_terms(bd, items)
    states = [s_ref[0], s_ref[1]]
    for step in range(GROUP):
        chunk = (step, GROUP - 1 - step)
        ups = []
        for d in range(2):
            qp, _, m_cat, _ = terms[d * GROUP + chunk[d]]
            ups.append(_dot3(jnp.concatenate([qp, m_cat], axis=0), _bdstack(states[d], bd)))
        for d in range(2):
            g = chunk[d]
            _, y_loc, _, g_cat = terms[d * GROUP + g]
            y_refs[d][g * CHUNK:(g + 1) * CHUNK, :] = ups[d][0:CHUNK] + y_loc
            states[d] = ups[d][CHUNK:] + g_cat
    s_ref[0] = states[0]
    s_ref[1] = states[1]


def _rwkv_call(rkv, z, conv, vecs, wup, aup, gup, t_ctx):
    b, l, _ = rkv.shape
    nb, nbc = l // BLOCK, t_ctx // BLOCK
    in_specs = []
    for d in range(2):
        in_specs += list(_block_specs(3 * C_MIX, 0, d, nbc, nb))
        in_specs.append(pl.BlockSpec((None, BLOCK, z.shape[2]),
                                     lambda i, j, d=d: (i, _block_of(j, d, nbc, nb), 0)))
    const = lambda a: pl.BlockSpec(a.shape, lambda i, j: (0,) * a.ndim)
    in_specs += [const(conv), const(vecs), const(wup), const(aup), const(gup)]
    out_d = lambda d: pl.BlockSpec((None, BLOCK, C_MIX), lambda i, j: (i, _block_of(j, d, nbc, nb), 0))
    shp = jax.ShapeDtypeStruct((b, l, C_MIX), F32)
    return pl.pallas_call(
        functools.partial(_rwkv_kernel, nbc, nb),
        grid=(b, nb),
        in_specs=in_specs,
        out_specs=[out_d(0), out_d(1), out_d(0), out_d(0)],
        out_shape=[shp, shp, shp, shp],
        scratch_shapes=[pltpu.VMEM((2, CHUNK, C_MIX), F32)],
        compiler_params=pltpu.CompilerParams(
            dimension_semantics=("arbitrary", "arbitrary"), vmem_limit_bytes=VMEM_LIMIT),
        name="rwkv7_chunked",
    )(rkv, rkv, rkv, z, rkv, rkv, rkv, z, conv, vecs, wup, aup, gup)


def _row_cumsum_matrix(d, bd):
    src = _iota((C_MIX, C_MIX), 0) & (CHUNK - 1)
    dst = _iota((C_MIX, C_MIX), 1) & (CHUNK - 1)
    keep = (src <= dst) if d == 0 else (src >= dst)
    return jnp.where(bd & keep, 1.0, 0.0)


def _log_sigmoid(x):
    return jnp.minimum(x, 0.0) - jnp.log1p(jnp.exp(-jnp.abs(x)))


def _mlstm_terms(bd, bd_ones, items):
    lane_h = _iota((CHUNK, C_MIX), 1) >> 6
    d_, q, k, v, b_exp, i_exp, b_row, i_row = [list(x) for x in zip(*items)]
    incl = [_tri_masks(d)[0] for d in d_]
    bend = [b[CHUNK - 1:CHUNK, :] if d == 0 else b[0:1, :] for d, b in zip(d_, b_exp)]
    logd = _each(lambda m, b, br, ir: jnp.where(m, b - br + ir, NEG_INF), incl, b_exp, b_row, i_row)

    def seg_max(x):
        out = jnp.zeros((CHUNK, C_MIX), F32)
        for h in range(N_HEADS):
            mx = jnp.max(jnp.where(lane_h == h, x, NEG_INF), axis=-1, keepdims=True)
            out = jnp.where(lane_h == h, mx, out)
        return out

    mloc = _each(seg_max, logd)
    qk = _each(lambda x, y: _dot_nt(x, _bdstack(y, bd)), q, k)
    sw = _each(lambda s, l, m: s * jnp.exp(l - m), qk, logd, mloc)
    num_loc = _each(lambda s, x: _dot(s, _bdstack(x, bd)), sw, v)
    den_loc = _each(lambda s: _dot(s, bd_ones), sw)
    g_col = _each(lambda e, b, i: e - b + i, bend, b_exp, i_exp)
    mg = _each(lambda g: jnp.max(g, axis=0, keepdims=True), g_col)
    kw = _each(lambda x, g, m: x * jnp.exp(g - m), k, g_col, mg)
    kg_cat = _each(lambda x, y: _fold(_dot_tn(x, y)), kw, v)
    kn = _each(lambda x: jnp.sum(x, axis=0, keepdims=True), kw)
    return list(zip(bend, mloc, num_loc, den_loc, mg, kg_cat, kn))


def _mlstm_kernel(nbc, nb, *refs):
    (x0, p0, n0, v0, g0, gr0, x1, p1, n1, v1, g1, gr1, conv_ref, bcol_ref, brow_ref,
     h0_ref, h1_ref, c_ref, n_ref, m_ref) = refs
    j = pl.program_id(1)

    @pl.when(j == 0)
    def _():
        c_ref[...] = jnp.zeros_like(c_ref)
        n_ref[...] = jnp.zeros_like(n_ref)
        m_ref[...] = jnp.zeros_like(m_ref)

    bd = _bd_mask()
    bd_ones = jnp.where(bd, 1.0, 0.0)
    ins = ((x0, p0, n0, v0, g0, gr0), (x1, p1, n1, v1, g1, gr1))
    h_refs = (h0_ref, h1_ref)
    items = []
    for d in range(2):
        x_ref, p_ref, nx_ref, v_ref, g_ref, gr_ref = ins[d]
        first, last = _stream_ends(_block_of(j, d, nbc, nb), nbc, nb)
        cv = _conv3(x_ref, p_ref, nx_ref, conv_ref, first, last)
        qk = cv * jax.nn.sigmoid(cv)
        q = qk[:, 0:C_MIX]
        k = qk[:, C_MIX:] * (HEAD_DIM ** -0.5)
        v = v_ref[...]
        gcol = g_ref[...] + bcol_ref[...]
        i_exp = _expand_cols(gcol, 4 * d)
        f_exp = _log_sigmoid(_expand_cols(gcol, 8 + 4 * d))
        b_exp = _dot_l01(_block_cumsum_matrix(d), f_exp)
        i_rows = gr_ref[d] + brow_ref[d]
        f_rows = _log_sigmoid(gr_ref[2 + d] + brow_ref[2 + d])
        f_pad = jnp.concatenate([f_rows] * (HALO // GROUP), axis=0)
        b_rows = _dot_r01_full(f_pad, _row_cumsum_matrix(d, bd))
        for g in range(GROUP):
            sl = slice(g * CHUNK, (g + 1) * CHUNK)
            items.append((d, q[sl], k[sl], v[sl], b_exp[sl], i_exp[sl], b_rows[g:g + 1, :], i_rows[g:g + 1, :]))
    terms = _mlstm_terms(bd, bd_ones, items)
    m_prev = [m_ref[0][0:1, :], m_ref[1][0:1, :]]
    n_prev = [n_ref[0][0:1, :], n_ref[1][0:1, :]]
    c_prev = [c_ref[0], c_ref[1]]
    for step in range(GROUP):
        chunk = (step, GROUP - 1 - step)
        for d in range(2):
            g = chunk[d]
            q, b_exp = items[d * GROUP + g][1], items[d * GROUP + g][4]
            bend, mloc, num_loc, den_loc, mg, kg_cat, kn = terms[d * GROUP + g]
            inter = b_exp + m_prev[d]
            m_t = jnp.maximum(inter, mloc)
            a_loc = jnp.exp(mloc - m_t)
            w_int = jnp.exp(inter - m_t)
            num = a_loc * num_loc + w_int * _dot(q, _bdstack(c_prev[d], bd))
            den = a_loc * den_loc + w_int * _dot(q * n_prev[d], bd_ones)
            h_refs[d][g * CHUNK:(g + 1) * CHUNK, :] = num / jnp.maximum(jnp.abs(den), jnp.exp(-m_t))
            m_new = jnp.maximum(bend + m_prev[d], mg)
            sc = jnp.exp(mg - m_new)
            dec = jnp.exp(bend + m_prev[d] - m_new)
            c_prev[d] = dec * c_prev[d] + sc * kg_cat
            n_prev[d] = dec * n_prev[d] + sc * kn
            m_prev[d] = m_new
    for d in range(2):
        c_ref[d] = c_prev[d]
        n_ref[d] = jnp.broadcast_to(n_prev[d], (HALO, C_MIX))
        m_ref[d] = jnp.broadcast_to(m_prev[d], (HALO, C_MIX))


def _mlstm_call(qk, vo, gt, grow, conv, bcol, brow, t_ctx):
    b, l, _ = qk.shape
    nb, nbc = l // BLOCK, t_ctx // BLOCK
    in_specs = []
    for d in range(2):
        bidx = lambda i, j, d=d: (i, _block_of(j, d, nbc, nb), 0)
        in_specs += list(_block_specs(2 * C_MIX, 0, d, nbc, nb))
        in_specs.append(pl.BlockSpec((None, BLOCK, C_MIX), bidx))
        in_specs.append(pl.BlockSpec((None, BLOCK, gt.shape[2]), bidx))
        in_specs.append(pl.BlockSpec((None, None, 4, GROUP, C_MIX),
                                     lambda i, j, d=d: (i, _block_of(j, d, nbc, nb), 0, 0, 0)))
    const = lambda a: pl.BlockSpec(a.shape, lambda i, j: (0,) * a.ndim)
    in_specs += [const(conv), const(bcol), const(brow)]
    out_d = lambda d: pl.BlockSpec((None, BLOCK, C_MIX), lambda i, j: (i, _block_of(j, d, nbc, nb), 0))
    shp = jax.ShapeDtypeStruct((b, l, C_MIX), F32)
    return pl.pallas_call(
        functools.partial(_mlstm_kernel, nbc, nb),
        grid=(b, nb),
        in_specs=in_specs,
        out_specs=[out_d(0), out_d(1)],
        out_shape=[shp, shp],
        scratch_shapes=[pltpu.VMEM((2, CHUNK, C_MIX), F32),
                        pltpu.VMEM((2, HALO, C_MIX), F32),
                        pltpu.VMEM((2, HALO, C_MIX), F32)],
        compiler_params=pltpu.CompilerParams(
            dimension_semantics=("arbitrary", "arbitrary"), vmem_limit_bytes=VMEM_LIMIT),
        name="mlstm_chunked",
    )(qk, qk, qk, vo, gt, grow, qk, qk, qk, vo, gt, grow, conv, bcol, brow)


def _attn_kernel(t_ctx, qb0, n_lat, sink_ref, q_ref, kc_ref, vc_ref, ka_ref, kb_ref, kd_ref,
                 va_ref, vb_ref, vd_ref, o_ref):
    qb = pl.program_id(1) + qb0
    ncb = t_ctx // ATTN_BLOCK
    lb = qb - ncb
    bq = ATTN_BLOCK
    n_heads = ATTN_KV_HEADS * ATTN_GROUP
    kmat = jnp.concatenate([kc_ref[...], ka_ref[...], kb_ref[...], kd_ref[...]], axis=0)
    vmat = jnp.concatenate([vc_ref[...], va_ref[...], vb_ref[...], vd_ref[...]], axis=0)
    nk = kmat.shape[0]

    lane_lo = _iota((bq, 2 * HEAD_DIM), 1) < HEAD_DIM
    scale = HEAD_DIM ** -0.5
    blocks = []
    for hq in range(n_heads):
        kvh = hq // ATTN_GROUP
        tile = q_ref[:, (hq // 2) * 128:(hq // 2 + 1) * 128] * scale
        if hq % 2 != kvh:
            tile = pltpu.roll(tile, HEAD_DIM, 1)
        blocks.append(jnp.where(lane_lo, tile, 0.0) if kvh == 0 else jnp.where(lane_lo, 0.0, tile))
    q_pos = _iota((bq, nk), 0) + bq
    col = _iota((bq, nk), 1)
    k_rel = col - t_ctx
    k_abs = k_rel + (lb - 1) * bq
    valid_loc = (jnp.abs(k_rel - q_pos) <= WINDOW) & (k_abs >= 0) & (k_abs < n_lat) & (lb >= 0)
    valid = (col < t_ctx) | valid_loc

    vaug = jnp.concatenate([vmat, jnp.ones_like(vmat)], axis=1)
    s_ = [jnp.where(valid, _dot_nt(blk, kmat), NEG_INF) for blk in blocks]
    sk_ = [sink_ref[h * bq:(h + 1) * bq, :] for h in range(n_heads)]
    m_ = [jnp.maximum(jnp.max(x, axis=-1, keepdims=True), k) for x, k in zip(s_, sk_)]
    e_ = [jnp.exp(x - m) for x, m in zip(s_, m_)]
    ov_ = [_dot(e, vaug) for e in e_]
    o_ = [ov[:, 0:128] / (ov[:, 128:256] + jnp.exp(k - m)) for ov, k, m in zip(ov_, sk_, m_)]
    for pair in range(n_heads // 2):
        kvh = (2 * pair) // ATTN_GROUP
        even = o_[2 * pair]
        odd = o_[2 * pair + 1]
        if kvh == 0:
            tile = jnp.where(lane_lo, even, pltpu.roll(odd, HEAD_DIM, 1))
        else:
            tile = jnp.where(lane_lo, pltpu.roll(even, HEAD_DIM, 1), odd)
        o_ref[:, pair * 128:(pair + 1) * 128] = tile


def _attn_call(qkv, sink, t_ctx, with_ctx_queries):
    b, l, _ = qkv.shape
    nqb = l // ATTN_BLOCK
    ncb = t_ctx // ATTN_BLOCK
    qb0 = 0 if with_ctx_queries else ncb
    kcol, vcol = C_A // 128, C_A // 128 + 1
    sink_rows = jnp.repeat(sink.astype(F32), ATTN_BLOCK).reshape(-1, 1)

    def loc(off, col):
        return pl.BlockSpec((None, ATTN_BLOCK, 128),
                            lambda i, t: (i, jnp.clip(t + qb0 + off, ncb, nqb - 1), col))

    return pl.pallas_call(
        functools.partial(_attn_kernel, t_ctx, qb0, l - t_ctx),
        grid=(b, nqb - qb0),
        in_specs=[pl.BlockSpec(sink_rows.shape, lambda i, t: (0, 0)),
                  pl.BlockSpec((None, ATTN_BLOCK, C_A), lambda i, t: (i, t + qb0, 0)),
                  pl.BlockSpec((None, t_ctx, 128), lambda i, t: (i, 0, kcol)),
                  pl.BlockSpec((None, t_ctx, 128), lambda i, t: (i, 0, vcol)),
                  loc(-1, kcol), loc(0, kcol), loc(1, kcol),
                  loc(-1, vcol), loc(0, vcol), loc(1, vcol)],
        out_specs=pl.BlockSpec((None, ATTN_BLOCK, C_A), lambda i, t: (i, t + qb0, 0)),
        out_shape=jax.ShapeDtypeStruct((b, l, C_A), F32),
        compiler_params=pltpu.CompilerParams(
            dimension_semantics=("arbitrary", "arbitrary"), vmem_limit_bytes=VMEM_LIMIT),
        name="attn_window",
    )(sink_rows, qkv, qkv, qkv, qkv, qkv, qkv, qkv, qkv, qkv)


def _outmlp_kernel(t_ctx, row0, final, x_ref, y0_ref, y1_ref, bonus_ref, gate_ref, ya_ref,
                   h0_ref, h1_ref, og_ref, mc_ref, ml_ref, g2_ref, lnv_ref, fin_ref,
                   wo_ref, w1_ref, w2_ref, o_ref):
    tm = x_ref.shape[0]
    row = row0 + pl.program_id(1) * tm + _iota((tm, 1), 0)
    is_ctx = row < t_ctx
    mod = lambda i: jnp.where(is_ctx, mc_ref[i:i + 1, :], ml_ref[i:i + 1, :])
    bdm = jnp.where(_bd_mask(), 1.0 / HEAD_DIM, 0.0)

    yr = _head_stats_norm(y0_ref[...] + y1_ref[...], bdm, RWKV_LN_EPS) * lnv_ref[0:1, :] + lnv_ref[1:2, :]
    yr = (yr + bonus_ref[...]) * gate_ref[...]
    ym = _head_stats_norm(h0_ref[...] + h1_ref[...], bdm, MLSTM_LN_EPS) * lnv_ref[2:3, :]
    ym = jax.nn.sigmoid(og_ref[...]) * ym
    mix = (_dot(yr.astype(BF16), wo_ref[0:C_MIX, :])
           + _dot(ya_ref[...].astype(BF16), wo_ref[C_MIX:C_MIX + C_A, :])
           + _dot(ym.astype(BF16), wo_ref[C_MIX + C_A:, :]))
    x1 = x_ref[...] + mod(0) * mix
    h2 = (_rms(x1, g2_ref[...]) * (1.0 + mod(2)) + mod(1)).astype(BF16)
    d_ff = w1_ref.shape[1]
    step = 1024
    acc = jnp.zeros_like(x1)
    for c0 in range(0, d_ff, step):
        hid = jnp.maximum(_dot(h2, w1_ref[:, c0:c0 + step]), 0.0)
        acc = acc + _dot((hid * hid).astype(BF16), w2_ref[c0:c0 + step, :])
    x2 = x1 + mod(3) * acc
    if final:
        x2 = _rms(x2, fin_ref[...])
    o_ref[...] = x2


def _outmlp_call(x, y0, y1, bonus, gate, ya, h0, h1, vo, modc, modl, g2, lnv, fin, wo, w1, w2,
                 t_ctx, tm, final):
    b, l, d = x.shape
    t0 = t_ctx // tm if final else 0
    n_t = l // tm - t0
    row = lambda n, col=0: pl.BlockSpec((None, tm, n), lambda i, t: (i, t + t0, col))
    const = lambda a: pl.BlockSpec(a.shape, lambda i, t: (0,) * a.ndim, pipeline_mode=pl.Buffered(1))
    small = lambda a: pl.BlockSpec(a.shape, lambda i, t: (0,) * a.ndim)
    out_rows = l - t0 * tm
    return pl.pallas_call(
        functools.partial(_outmlp_kernel, t_ctx, t0 * tm, final),
        grid=(b, n_t),
        in_specs=[row(d), row(C_MIX), row(C_MIX), row(C_MIX), row(C_MIX), row(C_A),
                  row(C_MIX), row(C_MIX), row(C_MIX, 1),
                  small(modc), pl.BlockSpec((None, 4, d), lambda i, t: (i, 0, 0)),
                  small(g2), small(lnv), small(fin), const(wo), const(w1), const(w2)],
        out_specs=pl.BlockSpec((None, tm, d), lambda i, t: (i, t, 0)),
        out_shape=jax.ShapeDtypeStruct((b, out_rows, d), F32),
        compiler_params=pltpu.CompilerParams(
            dimension_semantics=("arbitrary", "arbitrary"), vmem_limit_bytes=VMEM_LIMIT),
        name="outproj_mlp",
    )(x, y0, y1, bonus, gate, ya, h0, h1, vo, modc, modl, g2, lnv, fin, wo, w1, w2)


def _rope_tables(s, t_ctx):
    rows = s // GRID_W
    n_freq = HEAD_DIM // 4
    row = jnp.repeat(jnp.arange(rows), GRID_W).astype(F32)
    col = jnp.tile(jnp.arange(GRID_W), rows).astype(F32)
    inv = ROPE_BASE ** (-jnp.arange(n_freq, dtype=F32) / n_freq)
    ang = jnp.concatenate([row[:, None] * inv, col[:, None] * inv], axis=-1)
    cos, sin = jnp.cos(ang), jnp.sin(ang)
    cos_l = jnp.concatenate([cos, cos, cos, cos], axis=-1)
    sin_l = jnp.concatenate([-sin, sin, -sin, sin], axis=-1)
    cos_t = jnp.concatenate([jnp.ones((t_ctx, 128), F32), cos_l], axis=0)
    sin_t = jnp.concatenate([jnp.zeros((t_ctx, 128), F32), sin_l], axis=0)
    return cos_t, sin_t


def _pad_dir_lora(w):
    z = jnp.zeros_like(w[0])
    return jnp.stack([jnp.concatenate([w[0], z], axis=0), jnp.concatenate([z, w[1]], axis=0)], axis=0)


def kernel(x, c, ctx, c_ctx, ada_w, ada_b, norm1_g, norm2_g, w_in, w_out, rwkv_conv, rwkv_w0, rwkv_w_up, rwkv_a0, rwkv_a_up, rwkv_g_up, rwkv_k_k, rwkv_k_a, rwkv_r_k, rwkv_ln_w, rwkv_ln_b, attn_sink, mlstm_conv, mlstm_b_i, mlstm_b_f, mlstm_norm_g, mlp_w1, mlp_w2, final_g):
    b, s, d = x.shape
    t_ctx = ctx.shape[1]
    depth = ada_w.shape[0]
    l = t_ctx + s
    assert s % BLOCK == 0 and t_ctx % BLOCK == 0 and s % GRID_W == 0
    tm = 256
    assert l % tm == 0 and t_ctx % tm == 0

    n_mod_rows = -(-(b + 1) // HALO) * HALO
    cc = jnp.zeros((n_mod_rows, d), F32).at[:b].set(c).at[b].set(c_ctx)
    mods = _mods_call(cc, ada_w, ada_b).reshape(depth, n_mod_rows, 6, d)
    cos_t, sin_t = _rope_tables(s, t_ctx)
    xc = jnp.concatenate([ctx, x], axis=1)

    splits = [0, 3 * C_MIX]
    for w_ in (4 * R_LORA + R_GATE, C_A + 2 * C_KV, 2 * C_MIX, 2 * C_MIX, 4 * N_HEADS):
        splits.append(splits[-1] + w_)

    for li in range(depth):
        last = li == depth - 1
        m_l = mods[li, :b]
        m_c = mods[li, b]
        wl = w_in[li].astype(BF16)
        ws = [wl[:, splits[i]:splits[i + 1]] for i in range(6)]
        ws.append(jnp.transpose(ws[5]))
        rkv, z, qkv, qk, vo, gt, gtt = _inproj_call(
            xc, m_c[0:2], m_l[:, 0:2], norm1_g[li].reshape(1, d), cos_t, sin_t, ws, t_ctx, tm)

        vecs = jnp.stack([rwkv_k_k[li], rwkv_k_a[li], rwkv_r_k[li], rwkv_w0[li, 0], rwkv_w0[li, 1],
                          rwkv_a0[li, 0], rwkv_a0[li, 1], jnp.zeros((C_MIX,), F32)], axis=0)
        y0, y1, bonus, gate = _rwkv_call(rkv, z, rwkv_conv[li], vecs, _pad_dir_lora(rwkv_w_up[li]),
                                         _pad_dir_lora(rwkv_a_up[li]), rwkv_g_up[li], t_ctx)

        ya = _attn_call(qkv, attn_sink[li], t_ctx, not last)

        grow = (gtt.reshape(b, 4, N_HEADS, l // BLOCK, GROUP, CHUNK).transpose(0, 3, 1, 4, 2, 5)
                .reshape(b, l // BLOCK, 4, GROUP, C_MIX))
        bias = jnp.concatenate([mlstm_b_i[li].reshape(-1), mlstm_b_f[li].reshape(-1)])
        bcol = bias.reshape(1, 4 * N_HEADS)
        brow = jnp.repeat(bias.reshape(4, 1, N_HEADS), CHUNK, axis=2)
        h0, h1 = _mlstm_call(qk, vo, gt, grow, mlstm_conv[li], bcol, brow, t_ctx)

        lnv = jnp.stack([rwkv_ln_w[li], rwkv_ln_b[li], mlstm_norm_g[li], jnp.zeros((C_MIX,), F32)], axis=0)
        lnv = jnp.concatenate([lnv, jnp.zeros((4, C_MIX), F32)], axis=0)
        xc = _outmlp_call(xc, y0, y1, bonus, gate, ya, h0, h1, vo, m_c[2:6], m_l[:, 2:6],
                          norm2_g[li].reshape(1, d), lnv, final_g.reshape(1, d),
                          w_out[li].astype(BF16), mlp_w1[li].astype(BF16), mlp_w2[li].astype(BF16),
                          t_ctx, tm, last)
    return xc
```

```python
import functools

import jax
import jax.numpy as jnp
from jax import lax
from jax.experimental import pallas as pl
from jax.experimental.pallas import tpu as pltpu

F32 = jnp.float32
BF16 = jnp.bfloat16

HEAD_DIM = 64
N_HEADS = 4
C_MIX = N_HEADS * HEAD_DIM
ATTN_KV_HEADS = 2
ATTN_GROUP = 4
C_A = ATTN_KV_HEADS * ATTN_GROUP * HEAD_DIM
C_KV = ATTN_KV_HEADS * HEAD_DIM
GRID_W = 64
ATTN_BLOCK = 128
WINDOW = 128
ROPE_BASE = 10000.0
R_LORA = 64
R_GATE = 128
DECAY_SCALE = 0.606531
RMS_EPS = 1e-6
RWKV_LN_EPS = 64e-5
MLSTM_LN_EPS = 1e-6
NEG_INF = -1e30
CHUNK = 64
GROUP = 4
BLOCK = GROUP * CHUNK
HALO = 8
VMEM_LIMIT = 56 * 1024 * 1024


def _dot(a, b):
    return jnp.dot(a, b, preferred_element_type=F32)


def _dot_nt(a, b):
    return lax.dot_general(a, b, (((1,), (1,)), ((), ())), preferred_element_type=F32)


def _dot_tn(a, b):
    return lax.dot_general(a, b, (((0,), (0,)), ((), ())), preferred_element_type=F32)


def _split2(a):
    hi = a.astype(BF16)
    lo = (a - hi.astype(F32)).astype(BF16)
    return hi, lo


def _split3(a):
    a1 = a.astype(BF16)
    r1 = a - a1.astype(F32)
    a2 = r1.astype(BF16)
    a3 = (r1 - a2.astype(F32)).astype(BF16)
    return a1, a2, a3


def _dot_r01(a, m01):
    hi, lo = _split2(a)
    mb = m01.astype(BF16)
    return _dot(hi, mb) + _dot(lo, mb)


def _dot_l01(m01, a):
    a1, a2, a3 = _split3(a)
    mb = m01.astype(BF16)
    return _dot(mb, a1) + _dot(mb, a2) + _dot(mb, a3)


def _dot_r01_full(a, m01):
    a1, a2, a3 = _split3(a)
    mb = m01.astype(BF16)
    return _dot(a1, mb) + _dot(a2, mb) + _dot(a3, mb)


def _dot3(a, b):
    ah, al = _split2(a)
    bh, bl = _split2(b)
    return _dot(ah, bh) + _dot(al, bh) + _dot(ah, bl)


def _iota(shape, axis):
    return lax.broadcasted_iota(jnp.int32, shape, axis)


def _bd_mask():
    return (_iota((C_MIX, C_MIX), 0) >> 6) == (_iota((C_MIX, C_MIX), 1) >> 6)


def _bdstack(x, bd):
    return jnp.where(bd, jnp.concatenate([x, x, x, x], axis=0), 0.0)


def _fold(f):
    lane_h = _iota((CHUNK, C_MIX), 1) >> 6
    out = jnp.where(lane_h == 0, f[0:CHUNK], 0.0)
    for h in range(1, N_HEADS):
        out = jnp.where(lane_h == h, f[h * CHUNK:(h + 1) * CHUNK], out)
    return out


def _expand_cols(g, first):
    lane_h = _iota((g.shape[0], C_MIX), 1) >> 6
    out = jnp.zeros((g.shape[0], C_MIX), F32)
    for h in range(N_HEADS):
        out = jnp.where(lane_h == h, g[:, first + h:first + h + 1], out)
    return out


def _conv3(x_ref, prev_ref, next_ref, w_ref, first, last):
    xt = x_ref[...]
    n = xt.shape[0]
    rows = _iota((n, 1), 0)
    prev = prev_ref[HALO - 1:HALO, :] * jnp.where(first, 0.0, 1.0)
    nxt = next_ref[0:1, :] * jnp.where(last, 0.0, 1.0)
    up = jnp.where(rows == 0, prev, pltpu.roll(xt, 1, 0))
    dn = jnp.where(rows == n - 1, nxt, pltpu.roll(xt, n - 1, 0))
    return up * w_ref[0:1, :] + xt * w_ref[1:2, :] + dn * w_ref[2:3, :]


def _rms(x, g):
    return x * lax.rsqrt(jnp.mean(x * x, axis=-1, keepdims=True) + RMS_EPS) * g


def _head_stats_norm(y, bdm, eps):
    mu = _dot_r01(y, bdm)
    yc = y - mu
    var = _dot_r01(yc * yc, bdm)
    return yc * lax.rsqrt(var + eps)


def _mods_kernel(c_ref, w_ref, b_ref, o_ref):
    c = c_ref[...]
    s = c * jax.nn.sigmoid(c)
    o_ref[...] = _dot(s, w_ref[...]) + b_ref[...]


def _mods_call(cc, ada_w, ada_b):
    depth, d, n = ada_w.shape
    rows = cc.shape[0]
    tn = 1536
    return pl.pallas_call(
        _mods_kernel,
        grid=(depth, n // tn),
        in_specs=[pl.BlockSpec((rows, d), lambda l, j: (0, 0)),
                  pl.BlockSpec((None, d, tn), lambda l, j: (l, 0, j)),
                  pl.BlockSpec((None, 1, tn), lambda l, j: (l, 0, j))],
        out_specs=pl.BlockSpec((None, rows, tn), lambda l, j: (l, 0, j)),
        out_shape=jax.ShapeDtypeStruct((depth, rows, n), F32),
        compiler_params=pltpu.CompilerParams(vmem_limit_bytes=VMEM_LIMIT),
        name="ada_mods",
    )(cc, ada_w, ada_b.reshape(depth, 1, n))


def _inproj_kernel(t_ctx, x_ref, mc_ref, ml_ref, g_ref, cos_ref, sin_ref,
                   w_rkv, w_z, w_qkv, w_qk, w_vo, w_gt, w_gtt,
                   o_rkv, o_z, o_qkv, o_qk, o_vo, o_gt, o_gtt):
    tm = x_ref.shape[0]
    row = pl.program_id(1) * tm + _iota((tm, 1), 0)
    is_ctx = row < t_ctx
    shift = jnp.where(is_ctx, mc_ref[0:1, :], ml_ref[0:1, :])
    scale = jnp.where(is_ctx, mc_ref[1:2, :], ml_ref[1:2, :])
    h = _rms(x_ref[...], g_ref[...]) * (1.0 + scale) + shift
    hb = h.astype(BF16)
    o_rkv[...] = _dot(hb, w_rkv[...])
    o_z[...] = _dot(hb, w_z[...])
    o_qk[...] = _dot(hb, w_qk[...])
    o_vo[...] = _dot(hb, w_vo[...])
    o_gt[...] = _dot(hb, w_gt[...])
    o_gtt[...] = _dot_nt(w_gtt[...], hb)
    qkv = _dot(hb, w_qkv[...])
    cos = cos_ref[...]
    sin = sin_ref[...]
    first_half = (_iota((tm, 128), 1) & 63) < 32
    n_rot = (C_A + C_KV) // 128
    for i in range(n_rot):
        blk = qkv[:, i * 128:(i + 1) * 128]
        partner = jnp.where(first_half, pltpu.roll(blk, 96, 1), pltpu.roll(blk, 32, 1))
        o_qkv[:, i * 128:(i + 1) * 128] = blk * cos + partner * sin
    o_qkv[:, n_rot * 128:] = qkv[:, n_rot * 128:]


def _inproj_call(x, modc, modl, g, cos_t, sin_t, ws, t_ctx, tm):
    b, l, d = x.shape
    widths = [w.shape[1] for w in ws[:6]]
    row_spec = lambda n: pl.BlockSpec((None, tm, n), lambda i, t: (i, t, 0))
    const = lambda a: pl.BlockSpec(a.shape, lambda i, t: (0,) * a.ndim)
    out_shape = [jax.ShapeDtypeStruct((b, l, n), F32) for n in widths]
    out_shape.append(jax.ShapeDtypeStruct((b, widths[5], l), F32))
    out_specs = [row_spec(n) for n in widths]
    out_specs.append(pl.BlockSpec((None, widths[5], tm), lambda i, t: (i, 0, t)))
    return pl.pallas_call(
        functools.partial(_inproj_kernel, t_ctx),
        grid=(b, l // tm),
        in_specs=[row_spec(d), const(modc),
                  pl.BlockSpec((None, 2, d), lambda i, t: (i, 0, 0)),
                  const(g),
                  pl.BlockSpec((tm, 128), lambda i, t: (t, 0)),
                  pl.BlockSpec((tm, 128), lambda i, t: (t, 0))] + [const(w) for w in ws],
        out_specs=out_specs,
        out_shape=out_shape,
        compiler_params=pltpu.CompilerParams(
            dimension_semantics=("arbitrary", "arbitrary"), vmem_limit_bytes=VMEM_LIMIT),
        name="inproj",
    )(x, modc, modl, g, cos_t, sin_t, *ws)


def _block_of(j, d, nbc, nb):
    if d == 0:
        return j
    return jnp.where(j < nbc, nbc - 1 - j, nb + nbc - 1 - j)


def _block_specs(width, col, d, nbc, nb):
    per = BLOCK // HALO
    last_halo = nb * per - 1
    blk = pl.BlockSpec((None, BLOCK, width), lambda b, j: (b, _block_of(j, d, nbc, nb), col))
    prev = pl.BlockSpec((None, HALO, width),
                        lambda b, j: (b, jnp.maximum(_block_of(j, d, nbc, nb) * per - 1, 0), col))
    nxt = pl.BlockSpec((None, HALO, width),
                       lambda b, j: (b, jnp.minimum(_block_of(j, d, nbc, nb) * per + per, last_halo), col))
    return blk, prev, nxt


def _stream_ends(blk, nbc, nb):
    first = jnp.logical_or(blk == 0, blk == nbc)
    last = jnp.logical_or(blk == nbc - 1, blk == nb - 1)
    return first, last


def _tri_masks(d):
    t = _iota((CHUNK, C_MIX), 0)
    s = _iota((CHUNK, C_MIX), 1) & (CHUNK - 1)
    if d == 0:
        incl, strict = s <= t, s < t
    else:
        incl, strict = s >= t, s > t
    return incl, strict, s == t, (s >> 4) == (t >> 4)


def _block_cumsum_matrix(d):
    t = _iota((BLOCK, BLOCK), 0)
    s = _iota((BLOCK, BLOCK), 1)
    same = (t >> 6) == (s >> 6)
    keep = (s <= t) if d == 0 else (s >= t)
    return jnp.where(same & keep, 1.0, 0.0)


def _each(f, *lists):
    return [f(*xs) for xs in zip(*lists)]


def _rwkv_terms(bd, items):
    hmm = lambda ps, qs: [_dot(p, _bdstack(q, bd)) for p, q in zip(ps, qs)]
    plus = lambda xs, ys: _each(lambda x, y: x + y, xs, ys)
    minus = lambda xs, ys: _each(lambda x, y: x - y, xs, ys)
    times = lambda xs, ys: _each(lambda x, y: x * y, xs, ys)
    d_, r, kk, kd, bb, v, c, lw = [list(x) for x in zip(*items)]
    masks = [_tri_masks(d) for d in d_]
    incl = [m[0] for m in masks]
    strict = [m[1] for m in masks]
    eye_f = [jnp.where(m[2], 1.0, 0.0) for m in masks]
    blk16 = [m[3] for m in masks]
    ctot = [ci[CHUNK - 1:CHUNK, :] if d == 0 else ci[0:1, :] for d, ci in zip(d_, c)]
    rt = _each(lambda x, ci: x * jnp.exp(ci), r, c)
    at = _each(lambda x, ci, li: x * jnp.exp(ci - li), kk, c, lw)
    e_neg = _each(lambda ci: jnp.exp(-ci), c)
    kh = times(kd, e_neg)
    bh = times(bb, e_neg)
    e_end = _each(lambda ti, ci: jnp.exp(ti - ci), ctot, c)
    kbar = times(kd, e_end)
    bbar = times(bb, e_end)
    lhs = _each(lambda x, y: jnp.concatenate([x, y], axis=0), rt, at)
    a_k = _each(lambda x, y: _dot_nt(x, _bdstack(y, bd)), lhs, kh)
    a_b = _each(lambda x, y: _dot_nt(x, _bdstack(y, bd)), lhs, bh)
    a_rk = _each(lambda m, x: jnp.where(m, x[0:CHUNK], 0.0), incl, a_k)
    a_rb = _each(lambda m, x: jnp.where(m, x[0:CHUNK], 0.0), incl, a_b)
    a_ak = _each(lambda m, x: jnp.where(m, x[CHUNK:], 0.0), strict, a_k)
    nmat = _each(lambda m, x: jnp.where(m, x[CHUNK:], 0.0), strict, a_b)
    nd = _each(lambda m, x: jnp.where(m, x, 0.0), blk16, nmat)
    no = minus(nmat, nd)
    p2 = hmm(nd, nd)
    xd = hmm(minus(eye_f, nd), plus(eye_f, p2))
    p4 = hmm(p2, p2)
    xd = hmm(xd, plus(eye_f, p4))
    p8 = hmm(p4, p4)
    xd = hmm(xd, plus(eye_f, p8))
    pm = hmm(xd, no)
    pp = hmm(pm, pm)
    tinv = hmm(hmm(minus(eye_f, pm), plus(eye_f, pp)), xd)
    v_st = [_bdstack(x, bd) for x in v]
    w1 = hmm(tinv, at)
    ta = hmm(tinv, a_ak)
    uv = _each(lambda x, y, w: _dot(jnp.concatenate([x, y], axis=0), w), ta, a_rk, v_st)
    u = [x[0:CHUNK] for x in uv]
    qp = minus(rt, hmm(a_rb, w1))
    y_loc = minus([x[CHUNK:] for x in uv], hmm(a_rb, u))
    m_cat = _each(lambda e, t, x, y: e * jnp.exp(t) - _fold(_dot_tn(x, y)), eye_f, ctot, bbar, w1)
    g_cat = _each(lambda x, y, z, w: _fold(_dot_tn(x, y) - _dot_tn(z, w)), kbar, v, bbar, u)
    return list(zip(qp, y_loc, m_cat, g_cat))


def _rwkv_kernel(nbc, nb, *refs):
    (x0, p0, n0, z0, x1, p1, n1, z1, conv_ref, vec_ref, wup_ref, aup_ref, gup_ref,
     y0_ref, y1_ref, bonus_ref, gate_ref, s_ref) = refs
    j = pl.program_id(1)

    @pl.when(j == 0)
    def _():
        s_ref[...] = jnp.zeros_like(s_ref)

    bd = _bd_mask()
    bd_ones = jnp.where(bd, 1.0, 0.0)
    k_k = vec_ref[0:1, :]
    k_a = vec_ref[1:2, :]
    r_k = vec_ref[2:3, :]
    ins = ((x0, p0, n0, z0), (x1, p1, n1, z1))
    y_refs = (y0_ref, y1_ref)
    items = []
    for d in range(2):
        x_ref, p_ref, n_ref, z_ref = ins[d]
        first, last = _stream_ends(_block_of(j, d, nbc, nb), nbc, nb)
        cv = _conv3(x_ref, p_ref, n_ref, conv_ref, first, last)
        r = cv[:, 0:C_MIX]
        k = cv[:, C_MIX:2 * C_MIX]
        v = cv[:, 2 * C_MIX:3 * C_MIX]
        kk = k * k_k
        kk = kk * lax.rsqrt(_dot_r01(kk * kk, bd_ones) + 1e-12)
        zt = z_ref[...]
        zw = zt[:, 0:2 * R_LORA]
        za = zt[:, 2 * R_LORA:4 * R_LORA]
        dw = vec_ref[3 + d:4 + d, :] + _dot(jnp.tanh(zw), wup_ref[d])
        lw = -DECAY_SCALE * jax.nn.sigmoid(dw)
        a = jax.nn.sigmoid(vec_ref[5 + d:6 + d, :] + _dot(za, aup_ref[d]))
        kd = k * (1.0 + (a - 1.0) * k_a)
        bb = kk * a
        if d == 0:
            zg = zt[:, 4 * R_LORA:4 * R_LORA + R_GATE]
            gate_ref[...] = _dot(jax.nn.sigmoid(zg), gup_ref[...])
            bonus_ref[...] = _dot_r01(r * k * r_k, bd_ones) * v
        c = _dot_l01(_block_cumsum_matrix(d), lw)
        for g in range(GROUP):
            sl = slice(g * CHUNK, (g + 1) * CHUNK)
            items.append((d, r[sl], kk[sl], kd[sl], bb[sl], v[sl], c[sl], lw[sl]))
    terms = _rwkv_terms(bd, items)
    states = [s_ref[0], s_ref[1]]
    for step in range(GROUP):
        chunk = (step, GROUP - 1 - step)
        ups = []
        for d in range(2):
            qp, _, m_cat, _ = terms[d * GROUP + chunk[d]]
            ups.append(_dot3(jnp.concatenate([qp, m_cat], axis=0), _bdstack(states[d], bd)))
        for d in range(2):
            g = chunk[d]
            _, y_loc, _, g_cat = terms[d * GROUP + g]
            y_refs[d][g * CHUNK:(g + 1) * CHUNK, :] = ups[d][0:CHUNK] + y_loc
            states[d] = ups[d][CHUNK:] + g_cat
    s_ref[0] = states[0]
    s_ref[1] = states[1]


def _rwkv_call(rkv, z, conv, vecs, wup, aup, gup, t_ctx):
    b, l, _ = rkv.shape
    nb, nbc = l // BLOCK, t_ctx // BLOCK
    in_specs = []
    for d in range(2):
        in_specs += list(_block_specs(3 * C_MIX, 0, d, nbc, nb))
        in_specs.append(pl.BlockSpec((None, BLOCK, z.shape[2]),
                                     lambda i, j, d=d: (i, _block_of(j, d, nbc, nb), 0)))
    const = lambda a: pl.BlockSpec(a.shape, lambda i, j: (0,) * a.ndim)
    in_specs += [const(conv), const(vecs), const(wup), const(aup), const(gup)]
    out_d = lambda d: pl.BlockSpec((None, BLOCK, C_MIX), lambda i, j: (i, _block_of(j, d, nbc, nb), 0))
    shp = jax.ShapeDtypeStruct((b, l, C_MIX), F32)
    return pl.pallas_call(
        functools.partial(_rwkv_kernel, nbc, nb),
        grid=(b, nb),
        in_specs=in_specs,
        out_specs=[out_d(0), out_d(1), out_d(0), out_d(0)],
        out_shape=[shp, shp, shp, shp],
        scratch_shapes=[pltpu.VMEM((2, CHUNK, C_MIX), F32)],
        compiler_params=pltpu.CompilerParams(
            dimension_semantics=("arbitrary", "arbitrary"), vmem_limit_bytes=VMEM_LIMIT),
        name="rwkv7_chunked",
    )(rkv, rkv, rkv, z, rkv, rkv, rkv, z, conv, vecs, wup, aup, gup)


def _row_cumsum_matrix(d, bd):
    src = _iota((C_MIX, C_MIX), 0) & (CHUNK - 1)
    dst = _iota((C_MIX, C_MIX), 1) & (CHUNK - 1)
    keep = (src <= dst) if d == 0 else (src >= dst)
    return jnp.where(bd & keep, 1.0, 0.0)


def _log_sigmoid(x):
    return jnp.minimum(x, 0.0) - jnp.log1p(jnp.exp(-jnp.abs(x)))


def _mlstm_terms(bd, bd_ones, items):
    lane_h = _iota((CHUNK, C_MIX), 1) >> 6
    d_, q, k, v, b_exp, i_exp, b_row, i_row = [list(x) for x in zip(*items)]
    incl = [_tri_masks(d)[0] for d in d_]
    bend = [b[CHUNK - 1:CHUNK, :] if d == 0 else b[0:1, :] for d, b in zip(d_, b_exp)]
    logd = _each(lambda m, b, br, ir: jnp.where(m, b - br + ir, NEG_INF), incl, b_exp, b_row, i_row)

    def seg_max(x):
        out = jnp.zeros((CHUNK, C_MIX), F32)
        for h in range(N_HEADS):
            mx = jnp.max(jnp.where(lane_h == h, x, NEG_INF), axis=-1, keepdims=True)
            out = jnp.where(lane_h == h, mx, out)
        return out

    mloc = _each(seg_max, logd)
    qk = _each(lambda x, y: _dot_nt(x, _bdstack(y, bd)), q, k)
    sw = _each(lambda s, l, m: s * jnp.exp(l - m), qk, logd, mloc)
    num_loc = _each(lambda s, x: _dot(s, _bdstack(x, bd)), sw, v)
    den_loc = _each(lambda s: _dot(s, bd_ones), sw)
    g_col = _each(lambda e, b, i: e - b + i, bend, b_exp, i_exp)
    mg = _each(lambda g: jnp.max(g, axis=0, keepdims=True), g_col)
    kw = _each(lambda x, g, m: x * jnp.exp(g - m), k, g_col, mg)
    kg_cat = _each(lambda x, y: _fold(_dot_tn(x, y)), kw, v)
    kn = _each(lambda x: jnp.sum(x, axis=0, keepdims=True), kw)
    return list(zip(bend, mloc, num_loc, den_loc, mg, kg_cat, kn))


def _mlstm_kernel(nbc, nb, *refs):
    (x0, p0, n0, v0, g0, gr0, x1, p1, n1, v1, g1, gr1, conv_ref, bcol_ref, brow_ref,
     h0_ref, h1_ref, c_ref, n_ref, m_ref) = refs
    j = pl.program_id(1)

    @pl.when(j == 0)
    def _():
        c_ref[...] = jnp.zeros_like(c_ref)
        n_ref[...] = jnp.zeros_like(n_ref)
        m_ref[...] = jnp.zeros_like(m_ref)

    bd = _bd_mask()
    bd_ones = jnp.where(bd, 1.0, 0.0)
    ins = ((x0, p0, n0, v0, g0, gr0), (x1, p1, n1, v1, g1, gr1))
    h_refs = (h0_ref, h1_ref)
    items = []
    for d in range(2):
        x_ref, p_ref, nx_ref, v_ref, g_ref, gr_ref = ins[d]
        first, last = _stream_ends(_block_of(j, d, nbc, nb), nbc, nb)
        cv = _conv3(x_ref, p_ref, nx_ref, conv_ref, first, last)
        qk = cv * jax.nn.sigmoid(cv)
        q = qk[:, 0:C_MIX]
        k = qk[:, C_MIX:] * (HEAD_DIM ** -0.5)
        v = v_ref[...]
        gcol = g_ref[...] + bcol_ref[...]
        i_exp = _expand_cols(gcol, 4 * d)
        f_exp = _log_sigmoid(_expand_cols(gcol, 8 + 4 * d))
        b_exp = _dot_l01(_block_cumsum_matrix(d), f_exp)
        i_rows = gr_ref[d] + brow_ref[d]
        f_rows = _log_sigmoid(gr_ref[2 + d] + brow_ref[2 + d])
        f_pad = jnp.concatenate([f_rows] * (HALO // GROUP), axis=0)
        b_rows = _dot_r01_full(f_pad, _row_cumsum_matrix(d, bd))
        for g in range(GROUP):
            sl = slice(g * CHUNK, (g + 1) * CHUNK)
            items.append((d, q[sl], k[sl], v[sl], b_exp[sl], i_exp[sl], b_rows[g:g + 1, :], i_rows[g:g + 1, :]))
    terms = _mlstm_terms(bd, bd_ones, items)
    m_prev = [m_ref[0][0:1, :], m_ref[1][0:1, :]]
    n_prev = [n_ref[0][0:1, :], n_ref[1][0:1, :]]
    c_prev = [c_ref[0], c_ref[1]]
    for step in range(GROUP):
        chunk = (step, GROUP - 1 - step)
        for d in range(2):
            g = chunk[d]
            q, b_exp = items[d * GROUP + g][1], items[d * GROUP + g][4]
            bend, mloc, num_loc, den_loc, mg, kg_cat, kn = terms[d * GROUP + g]
            inter = b_exp + m_prev[d]
            m_t = jnp.maximum(inter, mloc)
            a_loc = jnp.exp(mloc - m_t)
            w_int = jnp.exp(inter - m_t)
            num = a_loc * num_loc + w_int * _dot(q, _bdstack(c_prev[d], bd))
            den = a_loc * den_loc + w_int * _dot(q * n_prev[d], bd_ones)
            h_refs[d][g * CHUNK:(g + 1) * CHUNK, :] = num / jnp.maximum(jnp.abs(den), jnp.exp(-m_t))
            m_new = jnp.maximum(bend + m_prev[d], mg)
            sc = jnp.exp(mg - m_new)
            dec = jnp.exp(bend + m_prev[d] - m_new)
            c_prev[d] = dec * c_prev[d] + sc * kg_cat
            n_prev[d] = dec * n_prev[d] + sc * kn
            m_prev[d] = m_new
    for d in range(2):
        c_ref[d] = c_prev[d]
        n_ref[d] = jnp.broadcast_to(n_prev[d], (HALO, C_MIX))
        m_ref[d] = jnp.broadcast_to(m_prev[d], (HALO, C_MIX))


def _mlstm_call(qk, vo, gt, grow, conv, bcol, brow, t_ctx):
    b, l, _ = qk.shape
    nb, nbc = l // BLOCK, t_ctx // BLOCK
    in_specs = []
    for d in range(2):
        bidx = lambda i, j, d=d: (i, _block_of(j, d, nbc, nb), 0)
        in_specs += list(_block_specs(2 * C_MIX, 0, d, nbc, nb))
        in_specs.append(pl.BlockSpec((None, BLOCK, C_MIX), bidx))
        in_specs.append(pl.BlockSpec((None, BLOCK, gt.shape[2]), bidx))
        in_specs.append(pl.BlockSpec((None, None, 4, GROUP, C_MIX),
                                     lambda i, j, d=d: (i, _block_of(j, d, nbc, nb), 0, 0, 0)))
    const = lambda a: pl.BlockSpec(a.shape, lambda i, j: (0,) * a.ndim)
    in_specs += [const(conv), const(bcol), const(brow)]
    out_d = lambda d: pl.BlockSpec((None, BLOCK, C_MIX), lambda i, j: (i, _block_of(j, d, nbc, nb), 0))
    shp = jax.ShapeDtypeStruct((b, l, C_MIX), F32)
    return pl.pallas_call(
        functools.partial(_mlstm_kernel, nbc, nb),
        grid=(b, nb),
        in_specs=in_specs,
        out_specs=[out_d(0), out_d(1)],
        out_shape=[shp, shp],
        scratch_shapes=[pltpu.VMEM((2, CHUNK, C_MIX), F32),
                        pltpu.VMEM((2, HALO, C_MIX), F32),
                        pltpu.VMEM((2, HALO, C_MIX), F32)],
        compiler_params=pltpu.CompilerParams(
            dimension_semantics=("arbitrary", "arbitrary"), vmem_limit_bytes=VMEM_LIMIT),
        name="mlstm_chunked",
    )(qk, qk, qk, vo, gt, grow, qk, qk, qk, vo, gt, grow, conv, bcol, brow)


def _attn_kernel(t_ctx, qb0, n_lat, sink_ref, q_ref, kc_ref, vc_ref, ka_ref, kb_ref, kd_ref,
                 va_ref, vb_ref, vd_ref, o_ref):
    qb = pl.program_id(1) + qb0
    ncb = t_ctx // ATTN_BLOCK
    lb = qb - ncb
    bq = ATTN_BLOCK
    n_heads = ATTN_KV_HEADS * ATTN_GROUP
    kmat = jnp.concatenate([kc_ref[...], ka_ref[...], kb_ref[...], kd_ref[...]], axis=0)
    vmat = jnp.concatenate([vc_ref[...], va_ref[...], vb_ref[...], vd_ref[...]], axis=0)
    nk = kmat.shape[0]

    lane_lo = _iota((bq, 2 * HEAD_DIM), 1) < HEAD_DIM
    scale = HEAD_DIM ** -0.5
    blocks = []
    for hq in range(n_heads):
        kvh = hq // ATTN_GROUP
        tile = q_ref[:, (hq // 2) * 128:(hq // 2 + 1) * 128] * scale
        if hq % 2 != kvh:
            tile = pltpu.roll(tile, HEAD_DIM, 1)
        blocks.append(jnp.where(lane_lo, tile, 0.0) if kvh == 0 else jnp.where(lane_lo, 0.0, tile))
    q_pos = _iota((bq, nk), 0) + bq
    col = _iota((bq, nk), 1)
    k_rel = col - t_ctx
    k_abs = k_rel + (lb - 1) * bq
    valid_loc = (jnp.abs(k_rel - q_pos) <= WINDOW) & (k_abs >= 0) & (k_abs < n_lat) & (lb >= 0)
    valid = (col < t_ctx) | valid_loc

    vaug = jnp.concatenate([vmat, jnp.ones_like(vmat)], axis=1)
    s_ = [jnp.where(valid, _dot_nt(blk, kmat), NEG_INF) for blk in blocks]
    sk_ = [sink_ref[h * bq:(h + 1) * bq, :] for h in range(n_heads)]
    m_ = [jnp.maximum(jnp.max(x, axis=-1, keepdims=True), k) for x, k in zip(s_, sk_)]
    e_ = [jnp.exp(x - m) for x, m in zip(s_, m_)]
    ov_ = [_dot(e, vaug) for e in e_]
    o_ = [ov[:, 0:128] / (ov[:, 128:256] + jnp.exp(k - m)) for ov, k, m in zip(ov_, sk_, m_)]
    for pair in range(n_heads // 2):
        kvh = (2 * pair) // ATTN_GROUP
        even = o_[2 * pair]
        odd = o_[2 * pair + 1]
        if kvh == 0:
            tile = jnp.where(lane_lo, even, pltpu.roll(odd, HEAD_DIM, 1))
        else:
            tile = jnp.where(lane_lo, pltpu.roll(even, HEAD_DIM, 1), odd)
        o_ref[:, pair * 128:(pair + 1) * 128] = tile


def _attn_call(qkv, sink, t_ctx, with_ctx_queries):
    b, l, _ = qkv.shape
    nqb = l // ATTN_BLOCK
    ncb = t_ctx // ATTN_BLOCK
    qb0 = 0 if with_ctx_queries else ncb
    kcol, vcol = C_A // 128, C_A // 128 + 1
    sink_rows = jnp.repeat(sink.astype(F32), ATTN_BLOCK).reshape(-1, 1)

    def loc(off, col):
        return pl.BlockSpec((None, ATTN_BLOCK, 128),
                            lambda i, t: (i, jnp.clip(t + qb0 + off, ncb, nqb - 1), col))

    return pl.pallas_call(
        functools.partial(_attn_kernel, t_ctx, qb0, l - t_ctx),
        grid=(b, nqb - qb0),
        in_specs=[pl.BlockSpec(sink_rows.shape, lambda i, t: (0, 0)),
                  pl.BlockSpec((None, ATTN_BLOCK, C_A), lambda i, t: (i, t + qb0, 0)),
                  pl.BlockSpec((None, t_ctx, 128), lambda i, t: (i, 0, kcol)),
                  pl.BlockSpec((None, t_ctx, 128), lambda i, t: (i, 0, vcol)),
                  loc(-1, kcol), loc(0, kcol), loc(1, kcol),
                  loc(-1, vcol), loc(0, vcol), loc(1, vcol)],
        out_specs=pl.BlockSpec((None, ATTN_BLOCK, C_A), lambda i, t: (i, t + qb0, 0)),
        out_shape=jax.ShapeDtypeStruct((b, l, C_A), F32),
        compiler_params=pltpu.CompilerParams(
            dimension_semantics=("arbitrary", "arbitrary"), vmem_limit_bytes=VMEM_LIMIT),
        name="attn_window",
    )(sink_rows, qkv, qkv, qkv, qkv, qkv, qkv, qkv, qkv, qkv)


def _outmlp_kernel(t_ctx, row0, final, x_ref, y0_ref, y1_ref, bonus_ref, gate_ref, ya_ref,
                   h0_ref, h1_ref, og_ref, mc_ref, ml_ref, g2_ref, lnv_ref, fin_ref,
                   wo_ref, w1_ref, w2_ref, o_ref):
    tm = x_ref.shape[0]
    row = row0 + pl.program_id(1) * tm + _iota((tm, 1), 0)
    is_ctx = row < t_ctx
    mod = lambda i: jnp.where(is_ctx, mc_ref[i:i + 1, :], ml_ref[i:i + 1, :])
    bdm = jnp.where(_bd_mask(), 1.0 / HEAD_DIM, 0.0)

    yr = _head_stats_norm(y0_ref[...] + y1_ref[...], bdm, RWKV_LN_EPS) * lnv_ref[0:1, :] + lnv_ref[1:2, :]
    yr = (yr + bonus_ref[...]) * gate_ref[...]
    ym = _head_stats_norm(h0_ref[...] + h1_ref[...], bdm, MLSTM_LN_EPS) * lnv_ref[2:3, :]
    ym = jax.nn.sigmoid(og_ref[...]) * ym
    mix = (_dot(yr.astype(BF16), wo_ref[0:C_MIX, :])
           + _dot(ya_ref[...].astype(BF16), wo_ref[C_MIX:C_MIX + C_A, :])
           + _dot(ym.astype(BF16), wo_ref[C_MIX + C_A:, :]))
    x1 = x_ref[...] + mod(0) * mix
    h2 = (_rms(x1, g2_ref[...]) * (1.0 + mod(2)) + mod(1)).astype(BF16)
    d_ff = w1_ref.shape[1]
    step = 1024
    acc = jnp.zeros_like(x1)
    for c0 in range(0, d_ff, step):
        hid = jnp.maximum(_dot(h2, w1_ref[:, c0:c0 + step]), 0.0)
        acc = acc + _dot((hid * hid).astype(BF16), w2_ref[c0:c0 + step, :])
    x2 = x1 + mod(3) * acc
    if final:
        x2 = _rms(x2, fin_ref[...])
    o_ref[...] = x2


def _outmlp_call(x, y0, y1, bonus, gate, ya, h0, h1, vo, modc, modl, g2, lnv, fin, wo, w1, w2,
                 t_ctx, tm, final):
    b, l, d = x.shape
    t0 = t_ctx // tm if final else 0
    n_t = l // tm - t0
    row = lambda n, col=0: pl.BlockSpec((None, tm, n), lambda i, t: (i, t + t0, col))
    const = lambda a: pl.BlockSpec(a.shape, lambda i, t: (0,) * a.ndim, pipeline_mode=pl.Buffered(1))
    small = lambda a: pl.BlockSpec(a.shape, lambda i, t: (0,) * a.ndim)
    out_rows = l - t0 * tm
    return pl.pallas_call(
        functools.partial(_outmlp_kernel, t_ctx, t0 * tm, final),
        grid=(b, n_t),
        in_specs=[row(d), row(C_MIX), row(C_MIX), row(C_MIX), row(C_MIX), row(C_A),
                  row(C_MIX), row(C_MIX), row(C_MIX, 1),
                  small(modc), pl.BlockSpec((None, 4, d), lambda i, t: (i, 0, 0)),
                  small(g2), small(lnv), small(fin), const(wo), const(w1), const(w2)],
        out_specs=pl.BlockSpec((None, tm, d), lambda i, t: (i, t, 0)),
        out_shape=jax.ShapeDtypeStruct((b, out_rows, d), F32),
        compiler_params=pltpu.CompilerParams(
            dimension_semantics=("arbitrary", "arbitrary"), vmem_limit_bytes=VMEM_LIMIT),
        name="outproj_mlp",
    )(x, y0, y1, bonus, gate, ya, h0, h1, vo, modc, modl, g2, lnv, fin, wo, w1, w2)


def _rope_tables(s, t_ctx):
    rows = s // GRID_W
    n_freq = HEAD_DIM // 4
    row = jnp.repeat(jnp.arange(rows), GRID_W).astype(F32)
    col = jnp.tile(jnp.arange(GRID_W), rows).astype(F32)
    inv = ROPE_BASE ** (-jnp.arange(n_freq, dtype=F32) / n_freq)
    ang = jnp.concatenate([row[:, None] * inv, col[:, None] * inv], axis=-1)
    cos, sin = jnp.cos(ang), jnp.sin(ang)
    cos_l = jnp.concatenate([cos, cos, cos, cos], axis=-1)
    sin_l = jnp.concatenate([-sin, sin, -sin, sin], axis=-1)
    cos_t = jnp.concatenate([jnp.ones((t_ctx, 128), F32), cos_l], axis=0)
    sin_t = jnp.concatenate([jnp.zeros((t_ctx, 128), F32), sin_l], axis=0)
    return cos_t, sin_t


def _pad_dir_lora(w):
    z = jnp.zeros_like(w[0])
    return jnp.stack([jnp.concatenate([w[0], z], axis=0), jnp.concatenate([z, w[1]], axis=0)], axis=0)


def kernel(x, c, ctx, c_ctx, ada_w, ada_b, norm1_g, norm2_g, w_in, w_out, rwkv_conv, rwkv_w0, rwkv_w_up, rwkv_a0, rwkv_a_up, rwkv_g_up, rwkv_k_k, rwkv_k_a, rwkv_r_k, rwkv_ln_w, rwkv_ln_b, attn_sink, mlstm_conv, mlstm_b_i, mlstm_b_f, mlstm_norm_g, mlp_w1, mlp_w2, final_g):
    b, s, d = x.shape
    t_ctx = ctx.shape[1]
    depth = ada_w.shape[0]
    l = t_ctx + s
    assert s % BLOCK == 0 and t_ctx % BLOCK == 0 and s % GRID_W == 0
    tm = 256
    assert l % tm == 0 and t_ctx % tm == 0

    n_mod_rows = -(-(b + 1) // HALO) * HALO
    cc = jnp.zeros((n_mod_rows, d), F32).at[:b].set(c).at[b].set(c_ctx)
    mods = _mods_call(cc, ada_w, ada_b).reshape(depth, n_mod_rows, 6, d)
    cos_t, sin_t = _rope_tables(s, t_ctx)
    xc = jnp.concatenate([ctx, x], axis=1)

    splits = [0, 3 * C_MIX]
    for w_ in (4 * R_LORA + R_GATE, C_A + 2 * C_KV, 2 * C_MIX, 2 * C_MIX, 4 * N_HEADS):
        splits.append(splits[-1] + w_)

    for li in range(depth):
        last = li == depth - 1
        m_l = mods[li, :b]
        m_c = mods[li, b]
        wl = w_in[li].astype(BF16)
        ws = [wl[:, splits[i]:splits[i + 1]] for i in range(6)]
        ws.append(jnp.transpose(ws[5]))
        rkv, z, qkv, qk, vo, gt, gtt = _inproj_call(
            xc, m_c[0:2], m_l[:, 0:2], norm1_g[li].reshape(1, d), cos_t, sin_t, ws, t_ctx, tm)

        vecs = jnp.stack([rwkv_k_k[li], rwkv_k_a[li], rwkv_r_k[li], rwkv_w0[li, 0], rwkv_w0[li, 1],
                          rwkv_a0[li, 0], rwkv_a0[li, 1], jnp.zeros((C_MIX,), F32)], axis=0)
        y0, y1, bonus, gate = _rwkv_call(rkv, z, rwkv_conv[li], vecs, _pad_dir_lora(rwkv_w_up[li]),
                                         _pad_dir_lora(rwkv_a_up[li]), rwkv_g_up[li], t_ctx)

        ya = _attn_call(qkv, attn_sink[li], t_ctx, not last)

        grow = (gtt.reshape(b, 4, N_HEADS, l // BLOCK, GROUP, CHUNK).transpose(0, 3, 1, 4, 2, 5)
                .reshape(b, l // BLOCK, 4, GROUP, C_MIX))
        bias = jnp.concatenate([mlstm_b_i[li].reshape(-1), mlstm_b_f[li].reshape(-1)])
        bcol = bias.reshape(1, 4 * N_HEADS)
        brow = jnp.repeat(bias.reshape(4, 1, N_HEADS), CHUNK, axis=2)
        h0, h1 = _mlstm_call(qk, vo, gt, grow, mlstm_conv[li], bcol, brow, t_ctx)

        lnv = jnp.stack([rwkv_ln_w[li], rwkv_ln_b[li], mlstm_norm_g[li], jnp.zeros((C_MIX,), F32)], axis=0)
        lnv = jnp.concatenate([lnv, jnp.zeros((4, C_MIX), F32)], axis=0)
        xc = _outmlp_call(xc, y0, y1, bonus, gate, ya, h0, h1, vo, m_c[2:6], m_l[:, 2:6],
                          norm2_g[li].reshape(1, d), lnv, final_g.reshape(1, d),
                          w_out[li].astype(BF16), mlp_w1[li].astype(BF16), mlp_w2[li].astype(BF16),
                          t_ctx, tm, last)
    return xc
```
